```python
import math
import jax, jax.numpy as jnp
from jax import lax
import numpy as np


D_MODEL = 2048
BATCH = 8
SEQ = 2048
DEPTH = 2
DEC_BATCH = 16
DEC_SEQ = 64
PAST_LEN = 1024

CHUNK = 64
N_A = DEPTH // 2
N_B = DEPTH - N_A
N_DENSE = (DEPTH + 1) // 2
N_MOE = DEPTH // 2
CONV_WIDTH = 31
HEAD_DIM = 64
N_HEADS = D_MODEL // HEAD_DIM
N_KV_HEADS = 8
GROUP = N_HEADS // N_KV_HEADS
WINDOW = 128
N_WIN_CHUNKS = WINDOW // CHUNK
BAND = WINDOW + CHUNK
NUM_BUCKETS = 32
MAX_DISTANCE = 128
D_FF = 5632
N_EXPERTS = 8
TOP_K = 2
D_EXPERT = 5632
MOE_BLOCK = 256
PLE_DIM = 256
ALPHA = (2 * DEPTH) ** 0.25
BETA = (8 * DEPTH) ** -0.25
LN_EPS = 1e-5
NEG_INF = -1e30

kernel_name = 'yoco_conformer_swa_sink_moe_stream_step'


def _layer_norm(x, g, b):
    xf = x.astype(jnp.float32)
    mu = xf.mean(-1, keepdims=True)
    var = jnp.square(xf - mu).mean(-1, keepdims=True)
    return ((xf - mu) * lax.rsqrt(var + LN_EPS) * g.astype(jnp.float32) + b.astype(jnp.float32)).astype(x.dtype)


def _conv_module(x, state, w_pw1, b_pw1, w_dw, b_dw, ln_g, ln_b, w_pw2, b_pw2):
    a, g = jnp.split(x @ w_pw1 + b_pw1, 2, axis=-1)
    u = a * jax.nn.sigmoid(g)
    ext = jnp.concatenate([state.astype(u.dtype), u], axis=1)
    y = lax.conv_general_dilated(ext, w_dw[:, None, :].astype(u.dtype), (1,), 'VALID',
                                 dimension_numbers=('NWC', 'WIO', 'NWC'),
                                 feature_group_count=D_MODEL) + b_dw
    y = jax.nn.silu(_layer_norm(y, ln_g, ln_b))
    return y @ w_pw2 + b_pw2, ext[:, ext.shape[1] - (CONV_WIDTH - 1):]


def _t5_bucket(rel):
    half = NUM_BUCKETS // 2
    exact = half // 2
    ret = jnp.where(rel > 0, half, 0)
    n = jnp.abs(rel)
    large = exact + (jnp.log(jnp.maximum(n, 1).astype(jnp.float32) / exact)
                     / math.log(MAX_DISTANCE / exact) * (half - exact)).astype(jnp.int32)
    large = jnp.minimum(large, half - 1)
    return ret + jnp.where(n < exact, n, large)


def _rel_bias(table, q_len):
    i = jnp.arange(q_len)[:, None]
    j = jnp.arange(WINDOW + q_len)[None, :]
    b = table.astype(jnp.float32)[_t5_bucket((j - WINDOW) - i)]
    return jnp.transpose(b, (2, 0, 1)).reshape(N_KV_HEADS, GROUP, q_len, WINDOW + q_len)


def _shared_kv(x, w_kv, past_k, past_v):
    B, S, _ = x.shape
    kv = (x @ w_kv).reshape(B, S, 2, N_KV_HEADS, HEAD_DIM)
    k, v = kv[:, :, 0], kv[:, :, 1]
    if past_k is None:
        nc = S // CHUNK

        def band(t):
            tp = jnp.pad(t, ((0, 0), (WINDOW, 0), (0, 0), (0, 0)))
            tp = tp.reshape(B, nc + N_WIN_CHUNKS, CHUNK, N_KV_HEADS, HEAD_DIM)
            return jnp.concatenate([tp[:, j:j + nc] for j in range(N_WIN_CHUNKS + 1)], axis=2)

        kpos = (jnp.arange(nc) * CHUNK)[:, None] - WINDOW + jnp.arange(BAND)[None, :]
        return band(k), band(v), kpos >= 0, k[:, S - WINDOW:], v[:, S - WINDOW:]
    k_all = jnp.concatenate([past_k.astype(k.dtype), k], axis=1)
    v_all = jnp.concatenate([past_v.astype(v.dtype), v], axis=1)
    valid = jnp.ones((1, WINDOW + S), bool)
    return k_all[:, None], v_all[:, None], valid, k_all[:, S:], v_all[:, S:]


def _window_attention(x, k_band, v_band, valid, bias, w_q, sinks, w_o, q_len):
    B, S, _ = x.shape
    q = (x @ w_q).reshape(B, S // q_len, q_len, N_KV_HEADS, GROUP, HEAD_DIM)
    s = jnp.einsum('bnqhgd,bnkhd->bnhgqk', q, k_band,
                   preferred_element_type=jnp.float32) * (HEAD_DIM ** -0.5) + bias
    s = jnp.where(valid[None, :, None, None, None, :], s, NEG_INF)
    sink = sinks.astype(jnp.float32).reshape(1, 1, N_KV_HEADS, GROUP, 1, 1)
    m = jnp.maximum(s.max(-1, keepdims=True), sink)
    e = jnp.exp(s - m)
    prob = e / (e.sum(-1, keepdims=True) + jnp.exp(sink - m))
    o = jnp.einsum('bnhgqk,bnkhd->bnqhgd', prob.astype(v_band.dtype), v_band)
    return o.reshape(B, S, N_HEADS * HEAD_DIM) @ w_o


def _swiglu(h, w_gate, w_up, w_down):
    return (jax.nn.silu(h @ w_gate) * (h @ w_up)) @ w_down


def _moe_swiglu(x2d, w_router, b_router, w_gate, w_up, w_down):
    T, D = x2d.shape
    A = T * TOP_K
    logits = jnp.dot(x2d, w_router, preferred_element_type=jnp.float32) + b_router.astype(jnp.float32)
    top_logit, top_e = lax.top_k(logits, TOP_K)
    gate = jax.nn.softmax(top_logit, axis=-1).reshape(A)
    flat_e = top_e.reshape(A)
    order = jnp.argsort(flat_e)
    sorted_e = flat_e[order]
    counts = jnp.bincount(flat_e, length=N_EXPERTS)
    padded = (counts + MOE_BLOCK - 1) // MOE_BLOCK * MOE_BLOCK
    pad_end = jnp.cumsum(padded)
    rank = jnp.arange(A) - (jnp.cumsum(counts) - counts)[sorted_e]
    dest = (pad_end - padded)[sorted_e] + rank
    n_blocks = -(-A // MOE_BLOCK) + N_EXPERTS - 1
    cap = n_blocks * MOE_BLOCK
    slot_tok = jnp.full((cap,), T, jnp.int32).at[dest].set((order // TOP_K).astype(jnp.int32))
    slot_gate = jnp.zeros((cap,), jnp.float32).at[dest].set(gate[order])
    block_e = jnp.minimum(jnp.searchsorted(pad_end, jnp.arange(n_blocks) * MOE_BLOCK, side='right'),
                          N_EXPERTS - 1)
    xs = jnp.concatenate([x2d, jnp.zeros((1, D), x2d.dtype)])[slot_tok].reshape(n_blocks, MOE_BLOCK, D)

    def expert_block(args):
        xb, e = args
        return (jax.nn.silu(xb @ w_gate[e]) * (xb @ w_up[e])) @ w_down[e]

    ys = lax.map(expert_block, (xs, block_e)).reshape(cap, D)
    ys = ys * slot_gate[:, None].astype(ys.dtype)
    return jnp.zeros((T + 1, D), x2d.dtype).at[slot_tok].add(ys.astype(x2d.dtype))[:T]


def _trunk(x, p, conv_state, past_k, past_v, prm):
    B, S, _ = x.shape
    q_len = CHUNK if past_k is None else S
    new_conv = []
    bands = None
    for layer in range(DEPTH):
        if layer < N_A:
            a = layer
            mix, st = _conv_module(x, conv_state[a], prm['conv_w_pw1'][a], prm['conv_b_pw1'][a],
                                   prm['conv_w_dw'][a], prm['conv_b_dw'][a], prm['conv_ln_g'][a],
                                   prm['conv_ln_b'][a], prm['conv_w_pw2'][a], prm['conv_b_pw2'][a])
            new_conv.append(st)
        else:
            if bands is None:
                bands = _shared_kv(x, prm['w_kv'], past_k, past_v)
                bias = _rel_bias(prm['rel_bias'], q_len)
            k_band, v_band, valid, k_win, v_win = bands
            bi = layer - N_A
            mix = _window_attention(x, k_band, v_band, valid, bias, prm['attn_w_q'][bi],
                                    prm['attn_sinks'][bi], prm['attn_w_o'][bi], q_len)
        h = _layer_norm(ALPHA * x + mix, prm['ln1_g'][layer], prm['ln1_b'][layer])
        if layer % 2 == 0:
            fi = layer // 2
            f = _swiglu(h, prm['ffn_w_gate'][fi], prm['ffn_w_up'][fi], prm['ffn_w_down'][fi])
        else:
            mi = layer // 2
            f = _moe_swiglu(h.reshape(B * S, D_MODEL), prm['moe_w_router'][mi], prm['moe_b_router'][mi],
                            prm['moe_w_gate'][mi], prm['moe_w_up'][mi],
                            prm['moe_w_down'][mi]).reshape(B, S, D_MODEL)
        ple = jax.nn.sigmoid(h @ prm['ple_w_gate'][layer]) * (p[layer].astype(h.dtype) @ prm['ple_w_proj'][layer])
        x = _layer_norm(ALPHA * h + f + ple, prm['ln2_g'][layer], prm['ln2_b'][layer])
    return x, jnp.stack(new_conv), k_win, v_win


def setup_inputs(seed: int = 0) -> dict:
    key = jax.random.key(seed)
    ks = jax.random.split(key, 40)
    d = D_MODEL
    hq = N_HEADS * HEAD_DIM
    hkv = 2 * N_KV_HEADS * HEAD_DIM

    def n(i, shape, scale):
        return jax.random.normal(ks[i], shape, jnp.float32) * scale

    return {
        'x_prompt': n(0, (BATCH, SEQ, d), 1.0),
        'x_sample': n(1, (DEC_BATCH, DEC_SEQ, d), 1.0),
        'state_conv': n(2, (N_A, DEC_BATCH, CONV_WIDTH - 1, d), 0.5),
        'cache_k': n(3, (DEC_BATCH, WINDOW, N_KV_HEADS, HEAD_DIM), 1.0),
        'cache_v': n(4, (DEC_BATCH, WINDOW, N_KV_HEADS, HEAD_DIM), 1.0),
        'p_prompt': n(5, (DEPTH, BATCH, SEQ, PLE_DIM), 1.0),
        'p_sample': n(6, (DEPTH, DEC_BATCH, DEC_SEQ, PLE_DIM), 1.0),
        'conv_w_pw1': n(7, (N_A, d, 2 * d), d ** -0.5),
        'conv_b_pw1': n(8, (N_A, 2 * d), 0.02),
        'conv_w_dw': n(9, (N_A, CONV_WIDTH, d), CONV_WIDTH ** -0.5),
        'conv_b_dw': n(10, (N_A, d), 0.02),
        'conv_ln_g': 1.0 + n(11, (N_A, d), 0.02),
        'conv_ln_b': n(12, (N_A, d), 0.02),
        'conv_w_pw2': n(13, (N_A, d, d), BETA * d ** -0.5),
        'conv_b_pw2': n(14, (N_A, d), 0.02),
        'w_kv': n(15, (d, hkv), d ** -0.5),
        'rel_bias': n(16, (NUM_BUCKETS, N_HEADS), 0.5),
        'attn_w_q': n(17, (N_B, d, hq), d ** -0.5),
        'attn_sinks': n(18, (N_B, N_HEADS), 0.5),
        'attn_w_o': n(19, (N_B, hq, d), BETA * hq ** -0.5),
        'ln1_g': 1.0 + n(20, (DEPTH, d), 0.02),
        'ln1_b': n(21, (DEPTH, d), 0.02),
        'ln2_g': 1.0 + n(22, (DEPTH, d), 0.02),
        'ln2_b': n(23, (DEPTH, d), 0.02),
        'ffn_w_gate': n(24, (N_DENSE, d, D_FF), d ** -0.5),
        'ffn_w_up': n(25, (N_DENSE, d, D_FF), d ** -0.5),
        'ffn_w_down': n(26, (N_DENSE, D_FF, d), BETA * D_FF ** -0.5),
        'moe_w_router': n(27, (N_MOE, d, N_EXPERTS), d ** -0.5),
        'moe_b_router': n(28, (N_MOE, N_EXPERTS), 0.01),
        'moe_w_gate': n(29, (N_MOE, N_EXPERTS, d, D_EXPERT), d ** -0.5),
        'moe_w_up': n(30, (N_MOE, N_EXPERTS, d, D_EXPERT), d ** -0.5),
        'moe_w_down': n(31, (N_MOE, N_EXPERTS, D_EXPERT, d), BETA * D_EXPERT ** -0.5),
        'ple_w_gate': n(32, (DEPTH, d, d), d ** -0.5),
        'ple_w_proj': n(33, (DEPTH, PLE_DIM, d), PLE_DIM ** -0.5),
    }


def reference(x_prompt, x_sample, state_conv, cache_k, cache_v, p_prompt, p_sample,
              conv_w_pw1, conv_b_pw1, conv_w_dw, conv_b_dw, conv_ln_g, conv_ln_b, conv_w_pw2, conv_b_pw2,
              w_kv, rel_bias, attn_w_q, attn_sinks, attn_w_o, ln1_g, ln1_b, ln2_g, ln2_b,
              ffn_w_gate, ffn_w_up, ffn_w_down, moe_w_router, moe_b_router, moe_w_gate, moe_w_up, moe_w_down,
              ple_w_gate, ple_w_proj):
    prm = dict(conv_w_pw1=conv_w_pw1, conv_b_pw1=conv_b_pw1, conv_w_dw=conv_w_dw, conv_b_dw=conv_b_dw,
               conv_ln_g=conv_ln_g, conv_ln_b=conv_ln_b, conv_w_pw2=conv_w_pw2, conv_b_pw2=conv_b_pw2,
               w_kv=w_kv, rel_bias=rel_bias, attn_w_q=attn_w_q, attn_sinks=attn_sinks, attn_w_o=attn_w_o,
               ln1_g=ln1_g, ln1_b=ln1_b, ln2_g=ln2_g, ln2_b=ln2_b,
               ffn_w_gate=ffn_w_gate, ffn_w_up=ffn_w_up, ffn_w_down=ffn_w_down,
               moe_w_router=moe_w_router, moe_b_router=moe_b_router, moe_w_gate=moe_w_gate,
               moe_w_up=moe_w_up, moe_w_down=moe_w_down, ple_w_gate=ple_w_gate, ple_w_proj=ple_w_proj)
    zero_conv = jnp.zeros((N_A, x_prompt.shape[0], CONV_WIDTH - 1, D_MODEL), x_prompt.dtype)
    y_prompt, conv_p, k_p, v_p = _trunk(x_prompt, p_prompt, zero_conv, None, None, prm)
    y_sample, conv_s, k_s, v_s = _trunk(x_sample, p_sample, state_conv, cache_k, cache_v, prm)
    return (y_prompt, y_sample, conv_p, conv_s, k_p, v_p, k_s, v_s)
```

```python
import functools
import math

import jax
import jax.numpy as jnp
from jax import lax
from jax.experimental import pallas as pl
from jax.experimental.pallas import tpu as pltpu

D_MODEL = 2048
CHUNK = 64
CONV_WIDTH = 31
HEAD_DIM = 64
N_HEADS = 32
N_KV_HEADS = 8
GROUP = N_HEADS // N_KV_HEADS
WINDOW = 128
BAND = WINDOW + CHUNK
NUM_BUCKETS = 32
MAX_DISTANCE = 128
D_FF = 5632
N_EXPERTS = 8
TOP_K = 2
PLE_DIM = 256
DEPTH = 2
ALPHA = (2 * DEPTH) ** 0.25
LN_EPS = 1e-5
NEG_INF = -1e30

KV_DIM = N_KV_HEADS * HEAD_DIM
CONV_HALO = 32
MOE_ROWS = 256
LANES = 128
MIB = 1024 * 1024

F32 = jnp.float32
BF16 = jnp.bfloat16


def _params(n_axes, vmem_mib):
    return pltpu.CompilerParams(dimension_semantics=("arbitrary",) * n_axes,
                                vmem_limit_bytes=vmem_mib * MIB)


def _dot(a, b):
    return jnp.dot(a, b, preferred_element_type=F32)


def _ln(x, g, b):
    mu = jnp.mean(x, axis=-1, keepdims=True)
    xc = x - mu
    var = jnp.mean(xc * xc, axis=-1, keepdims=True)
    return xc * lax.rsqrt(var + LN_EPS) * g + b


def _silu(x):
    return x * jax.nn.sigmoid(x)


def _pw1_glu_kernel(x_ref, wa_ref, wb_ref, ba_ref, bb_ref, u_ref, xb_ref):
    @pl.when(pl.program_id(1) == 0)
    def _():
        xb_ref[...] = x_ref[...].astype(BF16)

    xb = xb_ref[...]
    a = _dot(xb, wa_ref[...]) + ba_ref[...]
    g = _dot(xb, wb_ref[...]) + bb_ref[...]
    u_ref[...] = a * jax.nn.sigmoid(g)


def _pw1_glu(x, w, b):
    n = x.shape[0]
    bm, bn = math.gcd(n, 1024), 512
    nj = D_MODEL // bn
    return pl.pallas_call(
        _pw1_glu_kernel,
        grid=(n // bm, nj),
        in_specs=[pl.BlockSpec((bm, D_MODEL), lambda i, j: (i, 0)),
                  pl.BlockSpec((D_MODEL, bn), lambda i, j: (0, j)),
                  pl.BlockSpec((D_MODEL, bn), lambda i, j: (0, j + nj)),
                  pl.BlockSpec((1, bn), lambda i, j: (0, j)),
                  pl.BlockSpec((1, bn), lambda i, j: (0, j + nj))],
        out_specs=pl.BlockSpec((bm, bn), lambda i, j: (i, j)),
        out_shape=jax.ShapeDtypeStruct((n, D_MODEL), F32),
        scratch_shapes=[pltpu.VMEM((bm, D_MODEL), BF16)],
        compiler_params=_params(2, 48),
    )(x, w, w, b, b)


def _dwconv_kernel(prev_ref, cur_ref, w_ref, b_ref, y_ref, e_ref, *, bt, zero_first, rc):
    prev = prev_ref[...]
    if zero_first:
        prev = jnp.where(pl.program_id(1) == 0, 0.0, prev)
    e_ref[0:CONV_HALO, :] = prev
    e_ref[CONV_HALO:, :] = cur_ref[...]
    off = CONV_HALO - (CONV_WIDTH - 1)
    for r0 in range(0, bt, rc):
        acc = jnp.broadcast_to(b_ref[...], (rc, b_ref.shape[1]))
        for k in range(CONV_WIDTH):
            acc = acc + e_ref[r0 + k + off:r0 + k + off + rc, :] * w_ref[k:k + 1, :]
        y_ref[r0:r0 + rc, :] = acc


def _dwconv(u, prev_src, w, b, *, n_seq, seq, bt, zero_first):
    bc = 512
    nt = seq // bt
    hb = bt // CONV_HALO
    if zero_first:
        prev_map = lambda s, t, c: (jnp.maximum((s * nt + t) * hb - 1, 0), c)
    else:
        prev_map = lambda s, t, c: (s, c)
    return pl.pallas_call(
        functools.partial(_dwconv_kernel, bt=bt, zero_first=zero_first, rc=32),
        grid=(n_seq, nt, D_MODEL // bc),
        in_specs=[pl.BlockSpec((CONV_HALO, bc), prev_map),
                  pl.BlockSpec((bt, bc), lambda s, t, c: (s * nt + t, c)),
                  pl.BlockSpec((CONV_WIDTH, bc), lambda s, t, c: (0, c)),
                  pl.BlockSpec((1, bc), lambda s, t, c: (0, c))],
        out_specs=pl.BlockSpec((bt, bc), lambda s, t, c: (s * nt + t, c)),
        out_shape=jax.ShapeDtypeStruct(u.shape, F32),
        scratch_shapes=[pltpu.VMEM((bt + CONV_HALO, bc), F32)],
        compiler_params=_params(3, 32),
    )(prev_src, u, w, b)


def _pw2_ln1_kernel(y_ref, x_ref, w_ref, b2_ref, cg_ref, cb_ref, g_ref, b_ref, h_ref, hb_ref):
    y = _silu(_ln(y_ref[...], cg_ref[...], cb_ref[...]))
    mix = _dot(y.astype(BF16), w_ref[...]) + b2_ref[...]
    h = _ln(ALPHA * x_ref[...] + mix, g_ref[...], b_ref[...])
    h_ref[...] = h
    hb_ref[...] = h.astype(BF16)


def _pw2_ln1(y, x, w, b2, cg, cb, g, b):
    n = y.shape[0]
    bm = 256
    row = pl.BlockSpec((bm, D_MODEL), lambda i: (i, 0))
    vec = pl.BlockSpec((1, D_MODEL), lambda i: (0, 0))
    return pl.pallas_call(
        _pw2_ln1_kernel,
        grid=(n // bm,),
        in_specs=[row, row, pl.BlockSpec((D_MODEL, D_MODEL), lambda i: (0, 0)), vec, vec, vec, vec, vec],
        out_specs=[row, row],
        out_shape=[jax.ShapeDtypeStruct((n, D_MODEL), F32), jax.ShapeDtypeStruct((n, D_MODEL), BF16)],
        compiler_params=_params(1, 48),
    )(y, x, w, b2, cg, cb, g, b)


def _ffn_kernel(x_ref, wg_ref, wu_ref, wd_ref, o_ref):
    j = pl.program_id(1)
    x = x_ref[...]
    g = _dot(x, wg_ref[...])
    u = _dot(x, wu_ref[...])
    part = _dot((_silu(g) * u).astype(BF16), wd_ref[...])

    @pl.when(j == 0)
    def _():
        o_ref[...] = part

    @pl.when(j > 0)
    def _():
        o_ref[...] += part


def _ffn(xb, wg, wu, wd):
    n = xb.shape[0]
    bm, bf = math.gcd(n, 512), 512
    return pl.pallas_call(
        _ffn_kernel,
        grid=(n // bm, D_FF // bf),
        in_specs=[pl.BlockSpec((bm, D_MODEL), lambda i, j: (i, 0)),
                  pl.BlockSpec((D_MODEL, bf), lambda i, j: (0, j)),
                  pl.BlockSpec((D_MODEL, bf), lambda i, j: (0, j)),
                  pl.BlockSpec((bf, D_MODEL), lambda i, j: (j, 0))],
        out_specs=pl.BlockSpec((bm, D_MODEL), lambda i, j: (i, 0)),
        out_shape=jax.ShapeDtypeStruct((n, D_MODEL), F32),
        compiler_params=_params(2, 48),
    )(xb, wg, wu, wd)


def _ple_ln2_kernel(h_ref, hb_ref, f_ref, p_ref, wg_ref, wp_ref, g_ref, b_ref, x_ref, xb_ref):
    gate = jax.nn.sigmoid(_dot(hb_ref[...], wg_ref[...]))
    proj = _dot(p_ref[...].astype(BF16), wp_ref[...])
    x = _ln(ALPHA * h_ref[...] + f_ref[...] + gate * proj, g_ref[...], b_ref[...])
    x_ref[...] = x
    xb_ref[...] = x.astype(BF16)


def _ple_ln2(h, hb, f, p, wg, wp, g, b):
    n = h.shape[0]
    bm = 256
    row = pl.BlockSpec((bm, D_MODEL), lambda i: (i, 0))
    vec = pl.BlockSpec((1, D_MODEL), lambda i: (0, 0))
    return pl.pallas_call(
        _ple_ln2_kernel,
        grid=(n // bm,),
        in_specs=[row, row, row, pl.BlockSpec((bm, PLE_DIM), lambda i: (i, 0)),
                  pl.BlockSpec((D_MODEL, D_MODEL), lambda i: (0, 0)),
                  pl.BlockSpec((PLE_DIM, D_MODEL), lambda i: (0, 0)), vec, vec],
        out_specs=[row, row],
        out_shape=[jax.ShapeDtypeStruct((n, D_MODEL), F32), jax.ShapeDtypeStruct((n, D_MODEL), BF16)],
        compiler_params=_params(1, 48),
    )(h, hb, f, p, wg, wp, g, b)


def _mm_kernel(x_ref, w_ref, o_ref, *, scale):
    acc = _dot(x_ref[...], w_ref[...])
    if scale != 1.0:
        acc = acc * scale
    o_ref[...] = acc.astype(o_ref.dtype)


def _mm(xb, w, *, scale, out_dtype):
    n, k = xb.shape
    m = w.shape[1]
    bm, bn = math.gcd(n, 1024), 512
    return pl.pallas_call(
        functools.partial(_mm_kernel, scale=scale),
        grid=(n // bm, m // bn),
        in_specs=[pl.BlockSpec((bm, k), lambda i, j: (i, 0)),
                  pl.BlockSpec((k, bn), lambda i, j: (0, j))],
        out_specs=pl.BlockSpec((bm, bn), lambda i, j: (i, j)),
        out_shape=jax.ShapeDtypeStruct((n, m), out_dtype),
        compiler_params=_params(2, 32),
    )(xb, w)


def _t5_bucket(rel):
    half = NUM_BUCKETS // 2
    exact = half // 2
    ret = jnp.where(rel > 0, half, 0)
    n = jnp.abs(rel)
    large = exact + (jnp.log(jnp.maximum(n, 1).astype(F32) / exact)
                     / math.log(MAX_DISTANCE / exact) * (half - exact)).astype(jnp.int32)
    large = jnp.minimum(large, half - 1)
    return ret + jnp.where(n < exact, n, large)


def _bias_kernel(table_ref, bucket_ref, o_ref):
    bk = bucket_ref[...]
    masks = [bk == b for b in range(NUM_BUCKETS)]
    for h in range(N_HEADS):
        acc = jnp.zeros(bk.shape, F32)
        for b in range(NUM_BUCKETS):
            acc = jnp.where(masks[b], table_ref[b, h], acc)
        o_ref[h] = acc


def _rel_bias(table):
    i = jnp.arange(CHUNK)[:, None]
    j = jnp.arange(BAND)[None, :]
    bucket = _t5_bucket((j - WINDOW) - i).astype(jnp.int32)
    out = pl.pallas_call(
        _bias_kernel,
        in_specs=[pl.BlockSpec(memory_space=pltpu.SMEM), pl.BlockSpec(memory_space=pltpu.VMEM)],
        out_specs=pl.BlockSpec(memory_space=pltpu.VMEM),
        out_shape=jax.ShapeDtypeStruct((N_HEADS, CHUNK, BAND), F32),
    )(table.astype(F32), bucket)
    return out.reshape(N_KV_HEADS, GROUP * CHUNK, BAND)


def _attn_kernel(q_ref, kp_ref, vp_ref, kc_ref, vc_ref, bias_ref, sink_ref, o_ref, k_scr, v_scr, *, tq, mask_first):
    t = pl.program_id(1)
    k_scr[0:WINDOW, :] = kp_ref[...].astype(BF16)
    k_scr[WINDOW:, :] = kc_ref[...].astype(BF16)
    v_scr[0:WINDOW, :] = vp_ref[...].astype(BF16)
    v_scr[WINDOW:, :] = vc_ref[...].astype(BF16)
    rows = GROUP * CHUNK
    for h in range(N_KV_HEADS):
        hs = slice(h * HEAD_DIM, (h + 1) * HEAD_DIM)
        qs = slice(h * GROUP * HEAD_DIM, (h + 1) * GROUP * HEAD_DIM)
        bias_h = bias_ref[h]
        sink_h = sink_ref[h]
        for c in range(tq // CHUNK):
            cs = slice(c * CHUNK, (c + 1) * CHUNK)
            ks = slice(c * CHUNK, c * CHUNK + BAND)
            qc = q_ref[cs, qs]
            qh = jnp.concatenate([qc[:, g * HEAD_DIM:(g + 1) * HEAD_DIM] for g in range(GROUP)], axis=0)
            s = lax.dot_general(qh, k_scr[ks, hs], (((1,), (1,)), ((), ())), preferred_element_type=F32)
            s = s + bias_h
            if mask_first:
                first_valid = WINDOW - (t * tq + c * CHUNK)
                kidx = lax.broadcasted_iota(jnp.int32, (rows, BAND), 1)
                s = jnp.where(kidx >= first_valid, s, NEG_INF)
            m = jnp.maximum(jnp.max(s, axis=-1, keepdims=True), sink_h)
            e = jnp.exp(s - m)
            den = jnp.sum(e, axis=-1, keepdims=True) + jnp.exp(sink_h - m)
            p = (e / den).astype(BF16)
            oh = _dot(p, v_scr[ks, hs])
            o_ref[cs, qs] = jnp.concatenate(
                [oh[g * CHUNK:(g + 1) * CHUNK] for g in range(GROUP)], axis=1).astype(o_ref.dtype)


def _attention(q, kv, k_prev, v_prev, bias, sinks, *, n_seq, seq, tq, row0, mask_first):
    nt = seq // tq
    qb0 = row0 // tq
    if mask_first:
        wpt = tq // WINDOW
        kp_spec = pl.BlockSpec((WINDOW, KV_DIM), lambda s, t: (jnp.maximum((s * nt + t) * wpt - 1, 0), 0))
        vp_spec = pl.BlockSpec((WINDOW, KV_DIM), lambda s, t: (jnp.maximum((s * nt + t) * wpt - 1, 0), 1))
    else:
        kp_spec = pl.BlockSpec((WINDOW, KV_DIM), lambda s, t: (s, 0))
        vp_spec = pl.BlockSpec((WINDOW, KV_DIM), lambda s, t: (s, 0))
    return pl.pallas_call(
        functools.partial(_attn_kernel, tq=tq, mask_first=mask_first),
        grid=(n_seq, nt),
        in_specs=[pl.BlockSpec((tq, D_MODEL), lambda s, t: (qb0 + s * nt + t, 0)),
                  kp_spec, vp_spec,
                  pl.BlockSpec((tq, KV_DIM), lambda s, t: (qb0 + s * nt + t, 0)),
                  pl.BlockSpec((tq, KV_DIM), lambda s, t: (qb0 + s * nt + t, 1)),
                  pl.BlockSpec((N_KV_HEADS, GROUP * CHUNK, BAND), lambda s, t: (0, 0, 0)),
                  pl.BlockSpec((N_KV_HEADS, GROUP * CHUNK, 1), lambda s, t: (0, 0, 0))],
        out_specs=pl.BlockSpec((tq, D_MODEL), lambda s, t: (s * nt + t, 0)),
        out_shape=jax.ShapeDtypeStruct((n_seq * seq, D_MODEL), BF16),
        scratch_shapes=[pltpu.VMEM((WINDOW + tq, KV_DIM), BF16), pltpu.VMEM((WINDOW + tq, KV_DIM), BF16)],
        compiler_params=_params(2, 32),
    )(q, k_prev, v_prev, kv, kv, bias, sinks)


def _oproj_ln1_kernel(o_ref, x_ref, w_ref, g_ref, b_ref, h_ref):
    mix = _dot(o_ref[...], w_ref[...])
    h_ref[...] = _ln(ALPHA * x_ref[...] + mix, g_ref[...], b_ref[...])


def _oproj_ln1(o, x, w, g, b):
    n = o.shape[0]
    bm = 256
    row = pl.BlockSpec((bm, D_MODEL), lambda i: (i, 0))
    vec = pl.BlockSpec((1, D_MODEL), lambda i: (0, 0))
    return pl.pallas_call(
        _oproj_ln1_kernel,
        grid=(n // bm,),
        in_specs=[row, row, pl.BlockSpec((D_MODEL, D_MODEL), lambda i: (0, 0)), vec, vec],
        out_specs=row,
        out_shape=jax.ShapeDtypeStruct((n, D_MODEL), F32),
        compiler_params=_params(1, 48),
    )(o, x, w, g, b)


def _router_kernel(h_ref, wh_ref, wl_ref, b_ref, r_ref):
    h = h_ref[...]
    hh = h.astype(BF16)
    hl = (h - hh.astype(F32)).astype(BF16)
    logits = _dot(hh, wh_ref[...]) + (_dot(hl, wh_ref[...]) + _dot(hh, wl_ref[...])) + b_ref[...]
    lane = lax.broadcasted_iota(jnp.int32, logits.shape, 1).astype(F32)
    ninf = -jnp.inf
    lg = jnp.where(lane < N_EXPERTS, logits, ninf)
    m1 = jnp.max(lg, axis=-1, keepdims=True)
    i1 = jnp.min(jnp.where(lg == m1, lane, float(LANES)), axis=-1, keepdims=True)
    lg2 = jnp.where(lane == i1, ninf, lg)
    m2 = jnp.max(lg2, axis=-1, keepdims=True)
    i2 = jnp.min(jnp.where(lg2 == m2, lane, float(LANES)), axis=-1, keepdims=True)
    ex = jnp.exp(m2 - m1)
    den = 1.0 + ex
    out = jnp.where(lane == 0, i1, jnp.where(lane == 1, i2, jnp.where(lane == 2, 1.0 / den, ex / den)))
    r_ref[...] = jnp.where(lane < 4, out, 0.0)


def _router(h, w, b):
    n = h.shape[0]
    bm = math.gcd(n, 512)
    wp = jnp.zeros((D_MODEL, LANES), F32).at[:, :N_EXPERTS].set(w.astype(F32))
    wh = wp.astype(BF16)
    wl = (wp - wh.astype(F32)).astype(BF16)
    bp = jnp.zeros((1, LANES), F32).at[0, :N_EXPERTS].set(b.astype(F32))
    return pl.pallas_call(
        _router_kernel,
        grid=(n // bm,),
        in_specs=[pl.BlockSpec((bm, D_MODEL), lambda i: (i, 0)),
                  pl.BlockSpec((D_MODEL, LANES), lambda i: (0, 0)),
                  pl.BlockSpec((D_MODEL, LANES), lambda i: (0, 0)),
                  pl.BlockSpec((1, LANES), lambda i: (0, 0))],
        out_specs=pl.BlockSpec((bm, LANES), lambda i: (i, 0)),
        out_shape=jax.ShapeDtypeStruct((n, LANES), F32),
        compiler_params=_params(1, 32),
    )(h, wh, wl, bp)


def _gather_kernel(tok_ref, h_hbm, o_ref, buf, sem, *, rows, n_blocks):
    i = pl.program_id(0)

    def row_copy(tok, slot, r):
        return pltpu.make_async_copy(h_hbm.at[pl.ds(tok, 1)], buf.at[slot, pl.ds(r, 1)], sem.at[slot])

    def issue(blk, slot):
        def body(r, c):
            row_copy(tok_ref[blk * rows + r], slot, r).start()
            return c
        lax.fori_loop(0, rows, body, 0)

    @pl.when(i == 0)
    def _():
        issue(0, 0)

    @pl.when(i + 1 < n_blocks)
    def _():
        issue(i + 1, (i + 1) % 2)

    slot = i % 2

    def wait_body(r, c):
        row_copy(0, slot, r).wait()
        return c
    lax.fori_loop(0, rows, wait_body, 0)
    o_ref[...] = buf[slot].astype(o_ref.dtype)


def _gather_rows(slot_tok, h):
    cap = slot_tok.shape[0]
    rows = MOE_ROWS
    n_blocks = cap // rows
    return pl.pallas_call(
        functools.partial(_gather_kernel, rows=rows, n_blocks=n_blocks),
        grid_spec=pltpu.PrefetchScalarGridSpec(
            num_scalar_prefetch=1,
            grid=(n_blocks,),
            in_specs=[pl.BlockSpec(memory_space=pl.ANY)],
            out_specs=pl.BlockSpec((rows, D_MODEL), lambda i, tok: (i, 0)),
            scratch_shapes=[pltpu.VMEM((2, rows, D_MODEL), F32), pltpu.SemaphoreType.DMA((2,))]),
        out_shape=jax.ShapeDtypeStruct((cap, D_MODEL), BF16),
        compiler_params=_params(1, 32),
    )(slot_tok, h)


def _moe_up_kernel(be_ref, nu_ref, x_ref, wg_ref, wu_ref, a_ref):
    used = pl.program_id(1) < nu_ref[0]

    @pl.when(used)
    def _():
        x = x_ref[...]
        g = _dot(x, wg_ref[...])
        u = _dot(x, wu_ref[...])
        a_ref[...] = (_silu(g) * u).astype(a_ref.dtype)

    @pl.when(jnp.logical_not(used))
    def _():
        a_ref[...] = jnp.zeros(a_ref.shape, a_ref.dtype)


def _moe_up(block_e, n_used, xs, wg, wu):
    cap = xs.shape[0]
    bf = 512
    return pl.pallas_call(
        _moe_up_kernel,
        grid_spec=pltpu.PrefetchScalarGridSpec(
            num_scalar_prefetch=2,
            grid=(D_FF // bf, cap // MOE_ROWS),
            in_specs=[pl.BlockSpec((MOE_ROWS, D_MODEL), lambda j, i, be, nu: (i, 0)),
                      pl.BlockSpec((None, D_MODEL, bf), lambda j, i, be, nu: (be[i], 0, j)),
                      pl.BlockSpec((None, D_MODEL, bf), lambda j, i, be, nu: (be[i], 0, j))],
            out_specs=pl.BlockSpec((MOE_ROWS, bf), lambda j, i, be, nu: (i, j))),
        out_shape=jax.ShapeDtypeStruct((cap, D_FF), BF16),
        compiler_params=_params(2, 32),
    )(block_e, n_used, xs, wg, wu)


def _moe_down_kernel(be_ref, nu_ref, a_ref, wd_ref, y_ref):
    used = pl.program_id(1) < nu_ref[0]

    @pl.when(used)
    def _():
        y_ref[...] = _dot(a_ref[...], wd_ref[...])

    @pl.when(jnp.logical_not(used))
    def _():
        y_ref[...] = jnp.zeros(y_ref.shape, y_ref.dtype)


def _moe_down(block_e, n_used, a, wd):
    cap = a.shape[0]
    bn = 1024
    return pl.pallas_call(
        _moe_down_kernel,
        grid_spec=pltpu.PrefetchScalarGridSpec(
            num_scalar_prefetch=2,
            grid=(D_MODEL // bn, cap // MOE_ROWS),
            in_specs=[pl.BlockSpec((MOE_ROWS, D_FF), lambda j, i, be, nu: (i, 0)),
                      pl.BlockSpec((None, D_FF, bn), lambda j, i, be, nu: (be[i], 0, j))],
            out_specs=pl.BlockSpec((MOE_ROWS, bn), lambda j, i, be, nu: (i, j))),
        out_shape=jax.ShapeDtypeStruct((cap, D_MODEL), F32),
        compiler_params=_params(2, 48),
    )(block_e, n_used, a, wd)


def _combine_kernel(dest_ref, h_ref, r_ref, p_ref, wg_ref, wp_ref, g_ref, b_ref, ys_hbm, y_ref, buf, sem,
                    *, bm, row0, n_blocks):
    i = pl.program_id(0)

    def row_copy(src, slot, k, r):
        return pltpu.make_async_copy(ys_hbm.at[pl.ds(src, 1)], buf.at[slot, k, pl.ds(r, 1)], sem.at[slot])

    def issue(blk, slot):
        def body(r, c):
            a = TOP_K * (row0 + blk * bm + r)
            for k in range(TOP_K):
                row_copy(dest_ref[a + k], slot, k, r).start()
            return c
        lax.fori_loop(0, bm, body, 0)

    @pl.when(i == 0)
    def _():
        issue(0, 0)

    @pl.when(i + 1 < n_blocks)
    def _():
        issue(i + 1, (i + 1) % 2)

    slot = i % 2

    def wait_body(r, c):
        for k in range(TOP_K):
            row_copy(0, slot, k, r).wait()
        return c
    lax.fori_loop(0, bm, wait_body, 0)

    h = h_ref[...]
    r = r_ref[...]
    f = buf[slot, 0] * r[:, 2:3] + buf[slot, 1] * r[:, 3:4]
    gate = jax.nn.sigmoid(_dot(h.astype(BF16), wg_ref[...]))
    proj = _dot(p_ref[...].astype(BF16), wp_ref[...])
    y_ref[...] = _ln(ALPHA * h + f + gate * proj, g_ref[...], b_ref[...])


def _combine_ple_ln2(dest, h, route, p, ys, wg, wp, g, b, *, row0, n_rows):
    bm = 256
    n_blocks = n_rows // bm
    b0 = row0 // bm
    vec = pl.BlockSpec((1, D_MODEL), lambda i, d: (0, 0))
    return pl.pallas_call(
        functools.partial(_combine_kernel, bm=bm, row0=row0, n_blocks=n_blocks),
        grid_spec=pltpu.PrefetchScalarGridSpec(
            num_scalar_prefetch=1,
            grid=(n_blocks,),
            in_specs=[pl.BlockSpec((bm, D_MODEL), lambda i, d: (b0 + i, 0)),
                      pl.BlockSpec((bm, LANES), lambda i, d: (b0 + i, 0)),
                      pl.BlockSpec((bm, PLE_DIM), lambda i, d: (b0 + i, 0)),
                      pl.BlockSpec((D_MODEL, D_MODEL), lambda i, d: (0, 0)),
                      pl.BlockSpec((PLE_DIM, D_MODEL), lambda i, d: (0, 0)),
                      vec, vec,
                      pl.BlockSpec(memory_space=pl.ANY)],
            out_specs=pl.BlockSpec((bm, D_MODEL), lambda i, d: (i, 0)),
            scratch_shapes=[pltpu.VMEM((2, TOP_K, bm, D_MODEL), F32), pltpu.SemaphoreType.DMA((2,))]),
        out_shape=jax.ShapeDtypeStruct((n_rows, D_MODEL), F32),
        compiler_params=_params(1, 48),
    )(dest, h, route, p, wg, wp, g, b, ys)


def _route_plan(route):
    n = route.shape[0]
    n_assign = n * TOP_K
    flat_e = route[:, :TOP_K].astype(jnp.int32).reshape(n_assign)
    onehot = (flat_e[:, None] == jnp.arange(N_EXPERTS, dtype=jnp.int32)[None, :]).astype(jnp.int32)
    csum = jnp.cumsum(onehot, axis=0)
    counts = csum[-1]
    rank = jnp.sum(csum * onehot, axis=1) - 1
    padded = (counts + MOE_ROWS - 1) // MOE_ROWS * MOE_ROWS
    pad_end = jnp.cumsum(padded)
    dest = ((pad_end - padded)[flat_e] + rank).astype(jnp.int32)
    n_blocks = -(-n_assign // MOE_ROWS) + N_EXPERTS - 1
    cap = n_blocks * MOE_ROWS
    slot_tok = jnp.zeros((cap,), jnp.int32).at[dest].set(jnp.arange(n_assign, dtype=jnp.int32) // TOP_K)
    block_e = jnp.minimum(jnp.searchsorted(pad_end, jnp.arange(n_blocks) * MOE_ROWS, side='right'),
                          N_EXPERTS - 1).astype(jnp.int32)
    n_used = (pad_end[-1:] // MOE_ROWS).astype(jnp.int32)
    return dest, slot_tok, block_e, n_used


def kernel(x_prompt, x_sample, state_conv, cache_k, cache_v, p_prompt, p_sample, conv_w_pw1, conv_b_pw1, conv_w_dw, conv_b_dw, conv_ln_g, conv_ln_b, conv_w_pw2, conv_b_pw2, w_kv, rel_bias, attn_w_q, attn_sinks, attn_w_o, ln1_g, ln1_b, ln2_g, ln2_b, ffn_w_gate, ffn_w_up, ffn_w_down, moe_w_router, moe_b_router, moe_w_gate, moe_w_up, moe_w_down, ple_w_gate, ple_w_proj):
    n_pb, p_seq, _ = x_prompt.shape
    n_sb, s_seq, _ = x_sample.shape
    n_p, n_s = n_pb * p_seq, n_sb * s_seq
    vec = lambda v: v.reshape(1, -1).astype(F32)

    xp = x_prompt.reshape(n_p, D_MODEL)
    xs = x_sample.reshape(n_s, D_MODEL)
    w1 = conv_w_pw1[0].astype(BF16)
    b1 = vec(conv_b_pw1[0])
    u_p = _pw1_glu(xp, w1, b1)
    u_s = _pw1_glu(xs, w1, b1)
    w_dw, b_dw = conv_w_dw[0].astype(F32), vec(conv_b_dw[0])
    state = jnp.pad(state_conv[0], ((0, 0), (CONV_HALO - (CONV_WIDTH - 1), 0), (0, 0))).reshape(n_sb * CONV_HALO, D_MODEL)
    yc_p = _dwconv(u_p, u_p, w_dw, b_dw, n_seq=n_pb, seq=p_seq, bt=256, zero_first=True)
    yc_s = _dwconv(u_s, state, w_dw, b_dw, n_seq=n_sb, seq=s_seq, bt=s_seq, zero_first=False)
    w2 = conv_w_pw2[0].astype(BF16)
    tail0 = (w2, vec(conv_b_pw2[0]), vec(conv_ln_g[0]), vec(conv_ln_b[0]), vec(ln1_g[0]), vec(ln1_b[0]))
    h_p, hb_p = _pw2_ln1(yc_p, xp, *tail0)
    h_s, hb_s = _pw2_ln1(yc_s, xs, *tail0)
    h = jnp.concatenate([h_p, h_s], axis=0)
    hb = jnp.concatenate([hb_p, hb_s], axis=0)
    f = _ffn(hb, ffn_w_gate[0].astype(BF16), ffn_w_up[0].astype(BF16), ffn_w_down[0].astype(BF16))
    p0 = jnp.concatenate([p_prompt[0].reshape(n_p, PLE_DIM), p_sample[0].reshape(n_s, PLE_DIM)], axis=0)
    x1, x1b = _ple_ln2(h, hb, f, p0, ple_w_gate[0].astype(BF16), ple_w_proj[0].astype(BF16),
                       vec(ln2_g[0]), vec(ln2_b[0]))

    q = _mm(x1b, attn_w_q[0].astype(BF16), scale=HEAD_DIM ** -0.5, out_dtype=BF16)
    kv = _mm(x1b, w_kv.astype(BF16), scale=1.0, out_dtype=F32)
    bias = _rel_bias(rel_bias)
    sinks = jnp.repeat(attn_sinks[0].astype(F32), CHUNK).reshape(N_KV_HEADS, GROUP * CHUNK, 1)
    o_p = _attention(q, kv, kv, kv, bias, sinks, n_seq=n_pb, seq=p_seq, tq=256, row0=0, mask_first=True)
    o_s = _attention(q, kv, cache_k.reshape(n_sb * WINDOW, KV_DIM), cache_v.reshape(n_sb * WINDOW, KV_DIM),
                     bias, sinks, n_seq=n_sb, seq=s_seq, tq=s_seq, row0=n_p, mask_first=False)
    o = jnp.concatenate([o_p, o_s], axis=0)
    h1 = _oproj_ln1(o, x1, attn_w_o[0].astype(BF16), vec(ln1_g[1]), vec(ln1_b[1]))

    route = _router(h1, moe_w_router[0], moe_b_router[0])
    dest, slot_tok, block_e, n_used = _route_plan(route)
    xg = _gather_rows(slot_tok, h1)
    a = _moe_up(block_e, n_used, xg, moe_w_gate[0].astype(BF16), moe_w_up[0].astype(BF16))
    ys = _moe_down(block_e, n_used, a, moe_w_down[0].astype(BF16))
    p1 = jnp.concatenate([p_prompt[1].reshape(n_p, PLE_DIM), p_sample[1].reshape(n_s, PLE_DIM)], axis=0)
    tail1 = (ple_w_gate[1].astype(BF16), ple_w_proj[1].astype(BF16), vec(ln2_g[1]), vec(ln2_b[1]))
    y_p = _combine_ple_ln2(dest, h1, route, p1, ys, *tail1, row0=0, n_rows=n_p)
    y_s = _combine_ple_ln2(dest, h1, route, p1, ys, *tail1, row0=n_p, n_rows=n_s)

    keep = CONV_WIDTH - 1
    conv_p = u_p.reshape(n_pb, p_seq, D_MODEL)[None, :, p_seq - keep:]
    conv_s = u_s.reshape(n_sb, s_seq, D_MODEL)[None, :, s_seq - keep:]
    kv_p = kv[:n_p].reshape(n_pb, p_seq, 2, N_KV_HEADS, HEAD_DIM)[:, p_seq - WINDOW:]
    kv_s = kv[n_p:].reshape(n_sb, s_seq, 2, N_KV_HEADS, HEAD_DIM)
    k_s = jnp.concatenate([cache_k, kv_s[:, :, 0]], axis=1)[:, s_seq:]
    v_s = jnp.concatenate([cache_v, kv_s[:, :, 1]], axis=1)[:, s_seq:]
    return (y_p.reshape(n_pb, p_seq, D_MODEL), y_s.reshape(n_sb, s_seq, D_MODEL), conv_p, conv_s,
            kv_p[:, :, 0], kv_p[:, :, 1], k_s, v_s)
```

```python
import functools
import math

import jax
import jax.numpy as jnp
from jax import lax
from jax.experimental import pallas as pl
from jax.experimental.pallas import tpu as pltpu

D_MODEL = 2048
CHUNK = 64
CONV_WIDTH = 31
HEAD_DIM = 64
N_HEADS = 32
N_KV_HEADS = 8
GROUP = N_HEADS // N_KV_HEADS
WINDOW = 128
BAND = WINDOW + CHUNK
NUM_BUCKETS = 32
MAX_DISTANCE = 128
D_FF = 5632
N_EXPERTS = 8
TOP_K = 2
PLE_DIM = 256
DEPTH = 2
ALPHA = (2 * DEPTH) ** 0.25
LN_EPS = 1e-5
NEG_INF = -1e30

KV_DIM = N_KV_HEADS * HEAD_DIM
CONV_HALO = 32
MOE_ROWS = 512
DMA_ISSUE_UNROLL = 8
LANES = 128
SUBLANES = 8
MIB = 1024 * 1024

F32 = jnp.float32
BF16 = jnp.bfloat16


def _params(n_axes, vmem_mib):
    return pltpu.CompilerParams(dimension_semantics=("arbitrary",) * n_axes,
                                vmem_limit_bytes=vmem_mib * MIB)


def _dot(a, b):
    return jnp.dot(a, b, preferred_element_type=F32)


def _ln(x, g, b):
    mu = jnp.mean(x, axis=-1, keepdims=True)
    xc = x - mu
    var = jnp.mean(xc * xc, axis=-1, keepdims=True)
    return xc * lax.rsqrt(var + LN_EPS) * g + b


def _silu(x):
    return x * jax.nn.sigmoid(x)


def _pw1_glu_kernel(x_ref, wa_ref, wb_ref, ba_ref, bb_ref, u_ref, xb_ref):
    @pl.when(pl.program_id(1) == 0)
    def _():
        xb_ref[...] = x_ref[...].astype(BF16)

    xb = xb_ref[...]
    a = _dot(xb, wa_ref[...]) + ba_ref[...]
    g = _dot(xb, wb_ref[...]) + bb_ref[...]
    u_ref[...] = a * jax.nn.sigmoid(g)


def _pw1_glu(x, w, b):
    n = x.shape[0]
    bm, bn = math.gcd(n, 1024), 512
    nj = D_MODEL // bn
    return pl.pallas_call(
        _pw1_glu_kernel,
        grid=(n // bm, nj),
        in_specs=[pl.BlockSpec((bm, D_MODEL), lambda i, j: (i, 0)),
                  pl.BlockSpec((D_MODEL, bn), lambda i, j: (0, j)),
                  pl.BlockSpec((D_MODEL, bn), lambda i, j: (0, j + nj)),
                  pl.BlockSpec((1, bn), lambda i, j: (0, j)),
                  pl.BlockSpec((1, bn), lambda i, j: (0, j + nj))],
        out_specs=pl.BlockSpec((bm, bn), lambda i, j: (i, j)),
        out_shape=jax.ShapeDtypeStruct((n, D_MODEL), F32),
        scratch_shapes=[pltpu.VMEM((bm, D_MODEL), BF16)],
        compiler_params=_params(2, 48),
    )(x, w, w, b, b)


def _dwconv_kernel(prev_ref, cur_ref, w_ref, b_ref, y_ref, e_ref, *, bt, zero_first, rc):
    prev = prev_ref[...]
    if zero_first:
        prev = jnp.where(pl.program_id(1) == 0, 0.0, prev)
    e_ref[0, 0:CONV_HALO, :] = prev
    e_ref[0, CONV_HALO:, :] = cur_ref[...]
    e0 = e_ref[0]
    n_e = bt + CONV_HALO
    for rho in range(1, SUBLANES):
        e_ref[rho] = pltpu.roll(e0, n_e - rho, axis=0)
    off = CONV_HALO - (CONV_WIDTH - 1)
    for r0 in range(0, bt, rc):
        acc = jnp.broadcast_to(b_ref[...], (rc, b_ref.shape[1]))
        for k in range(CONV_WIDTH):
            rho = (k + off) % SUBLANES
            base = r0 + k + off - rho
            acc = acc + e_ref[rho, base:base + rc, :] * w_ref[k:k + 1, :]
        y_ref[r0:r0 + rc, :] = acc


def _dwconv(u, prev_src, w, b, *, n_seq, seq, bt, zero_first):
    bc = 512
    nt = seq // bt
    hb = bt // CONV_HALO
    if zero_first:
        prev_map = lambda s, t, c: (jnp.maximum((s * nt + t) * hb - 1, 0), c)
    else:
        prev_map = lambda s, t, c: (s, c)
    return pl.pallas_call(
        functools.partial(_dwconv_kernel, bt=bt, zero_first=zero_first, rc=32),
        grid=(n_seq, nt, D_MODEL // bc),
        in_specs=[pl.BlockSpec((CONV_HALO, bc), prev_map),
                  pl.BlockSpec((bt, bc), lambda s, t, c: (s * nt + t, c)),
                  pl.BlockSpec((CONV_WIDTH, bc), lambda s, t, c: (0, c)),
                  pl.BlockSpec((1, bc), lambda s, t, c: (0, c))],
        out_specs=pl.BlockSpec((bt, bc), lambda s, t, c: (s * nt + t, c)),
        out_shape=jax.ShapeDtypeStruct(u.shape, F32),
        scratch_shapes=[pltpu.VMEM((SUBLANES, bt + CONV_HALO, bc), F32)],
        compiler_params=_params(3, 32),
    )(prev_src, u, w, b)


def _pw2_ln1_kernel(y_ref, x_ref, w_ref, b2_ref, cg_ref, cb_ref, g_ref, b_ref, h_ref, hb_ref):
    y = _silu(_ln(y_ref[...], cg_ref[...], cb_ref[...]))
    mix = _dot(y.astype(BF16), w_ref[...]) + b2_ref[...]
    h = _ln(ALPHA * x_ref[...] + mix, g_ref[...], b_ref[...])
    h_ref[...] = h
    hb_ref[...] = h.astype(BF16)


def _pw2_ln1(y, x, w, b2, cg, cb, g, b):
    n = y.shape[0]
    bm = 256
    row = pl.BlockSpec((bm, D_MODEL), lambda i: (i, 0))
    vec = pl.BlockSpec((1, D_MODEL), lambda i: (0, 0))
    return pl.pallas_call(
        _pw2_ln1_kernel,
        grid=(n // bm,),
        in_specs=[row, row, pl.BlockSpec((D_MODEL, D_MODEL), lambda i: (0, 0)), vec, vec, vec, vec, vec],
        out_specs=[row, row],
        out_shape=[jax.ShapeDtypeStruct((n, D_MODEL), F32), jax.ShapeDtypeStruct((n, D_MODEL), BF16)],
        compiler_params=_params(1, 48),
    )(y, x, w, b2, cg, cb, g, b)


def _ffn_kernel(x_ref, wg_ref, wu_ref, wd_ref, o_ref):
    j = pl.program_id(1)
    x = x_ref[...]
    g = _dot(x, wg_ref[...])
    u = _dot(x, wu_ref[...])
    part = _dot((_silu(g) * u).astype(BF16), wd_ref[...])

    @pl.when(j == 0)
    def _():
        o_ref[...] = part

    @pl.when(j > 0)
    def _():
        o_ref[...] += part


def _ffn(xb, wg, wu, wd):
    n = xb.shape[0]
    bm, bf = math.gcd(n, 512), 512
    return pl.pallas_call(
        _ffn_kernel,
        grid=(n // bm, D_FF // bf),
        in_specs=[pl.BlockSpec((bm, D_MODEL), lambda i, j: (i, 0)),
                  pl.BlockSpec((D_MODEL, bf), lambda i, j: (0, j)),
                  pl.BlockSpec((D_MODEL, bf), lambda i, j: (0, j)),
                  pl.BlockSpec((bf, D_MODEL), lambda i, j: (j, 0))],
        out_specs=pl.BlockSpec((bm, D_MODEL), lambda i, j: (i, 0)),
        out_shape=jax.ShapeDtypeStruct((n, D_MODEL), F32),
        compiler_params=_params(2, 48),
    )(xb, wg, wu, wd)


def _ple_ln2_kernel(h_ref, hb_ref, f_ref, p_ref, wg_ref, wp_ref, g_ref, b_ref, x_ref, xb_ref):
    gate = jax.nn.sigmoid(_dot(hb_ref[...], wg_ref[...]))
    proj = _dot(p_ref[...].astype(BF16), wp_ref[...])
    x = _ln(ALPHA * h_ref[...] + f_ref[...] + gate * proj, g_ref[...], b_ref[...])
    x_ref[...] = x
    xb_ref[...] = x.astype(BF16)


def _ple_ln2(h, hb, f, p, wg, wp, g, b):
    n = h.shape[0]
    bm = 256
    row = pl.BlockSpec((bm, D_MODEL), lambda i: (i, 0))
    vec = pl.BlockSpec((1, D_MODEL), lambda i: (0, 0))
    return pl.pallas_call(
        _ple_ln2_kernel,
        grid=(n // bm,),
        in_specs=[row, row, row, pl.BlockSpec((bm, PLE_DIM), lambda i: (i, 0)),
                  pl.BlockSpec((D_MODEL, D_MODEL), lambda i: (0, 0)),
                  pl.BlockSpec((PLE_DIM, D_MODEL), lambda i: (0, 0)), vec, vec],
        out_specs=[row, row],
        out_shape=[jax.ShapeDtypeStruct((n, D_MODEL), F32), jax.ShapeDtypeStruct((n, D_MODEL), BF16)],
        compiler_params=_params(1, 48),
    )(h, hb, f, p, wg, wp, g, b)


def _mm_kernel(x_ref, w_ref, o_ref, *, scale):
    acc = _dot(x_ref[...], w_ref[...])
    if scale != 1.0:
        acc = acc * scale
    o_ref[...] = acc.astype(o_ref.dtype)


def _mm(xb, w, *, scale, out_dtype):
    n, k = xb.shape
    m = w.shape[1]
    bm, bn = math.gcd(n, 1024), 512
    return pl.pallas_call(
        functools.partial(_mm_kernel, scale=scale),
        grid=(n // bm, m // bn),
        in_specs=[pl.BlockSpec((bm, k), lambda i, j: (i, 0)),
                  pl.BlockSpec((k, bn), lambda i, j: (0, j))],
        out_specs=pl.BlockSpec((bm, bn), lambda i, j: (i, j)),
        out_shape=jax.ShapeDtypeStruct((n, m), out_dtype),
        compiler_params=_params(2, 32),
    )(xb, w)


def _t5_bucket(rel):
    half = NUM_BUCKETS // 2
    exact = half // 2
    ret = jnp.where(rel > 0, half, 0)
    n = jnp.abs(rel)
    large = exact + (jnp.log(jnp.maximum(n, 1).astype(F32) / exact)
                     / math.log(MAX_DISTANCE / exact) * (half - exact)).astype(jnp.int32)
    large = jnp.minimum(large, half - 1)
    return ret + jnp.where(n < exact, n, large)


def _bias_kernel(table_ref, bucket_ref, o_ref):
    bk = bucket_ref[...]
    masks = [bk == b for b in range(NUM_BUCKETS)]
    for h in range(N_HEADS):
        acc = jnp.zeros(bk.shape, F32)
        for b in range(NUM_BUCKETS):
            acc = jnp.where(masks[b], table_ref[b, h], acc)
        o_ref[h] = acc


def _rel_bias(table):
    i = jnp.arange(CHUNK)[:, None]
    j = jnp.arange(BAND)[None, :]
    bucket = _t5_bucket((j - WINDOW) - i).astype(jnp.int32)
    out = pl.pallas_call(
        _bias_kernel,
        in_specs=[pl.BlockSpec(memory_space=pltpu.SMEM), pl.BlockSpec(memory_space=pltpu.VMEM)],
        out_specs=pl.BlockSpec(memory_space=pltpu.VMEM),
        out_shape=jax.ShapeDtypeStruct((N_HEADS, CHUNK, BAND), F32),
    )(table.astype(F32), bucket)
    return out.reshape(N_KV_HEADS, GROUP * CHUNK, BAND)


def _attn_kernel(q_ref, kp_ref, vp_ref, kc_ref, vc_ref, bias_ref, sink_ref, o_ref, q_scr, k_scr, v_scr,
                 *, tq, mask_first):
    t = pl.program_id(1)
    for h in range(N_KV_HEADS):
        hs = slice(h * HEAD_DIM, (h + 1) * HEAD_DIM)
        k_scr[h, 0:WINDOW, :] = kp_ref[:, hs].astype(BF16)
        k_scr[h, WINDOW:, :] = kc_ref[:, hs].astype(BF16)
        v_scr[h, 0:WINDOW, :] = vp_ref[:, hs].astype(BF16)
        v_scr[h, WINDOW:, :] = vc_ref[:, hs].astype(BF16)
    for h in range(N_HEADS):
        q_scr[h] = q_ref[:, h * HEAD_DIM:(h + 1) * HEAD_DIM]
    rows = GROUP * CHUNK
    bias = bias_ref[...]
    sink = sink_ref[...]
    for c in range(tq // CHUNK):
        cs = slice(c * CHUNK, (c + 1) * CHUNK)
        ks = slice(c * CHUNK, c * CHUNK + BAND)
        qh = q_scr[:, cs, :].reshape(N_KV_HEADS, rows, HEAD_DIM)
        s = jnp.einsum('hqd,hkd->hqk', qh, k_scr[:, ks, :], preferred_element_type=F32) + bias
        if mask_first:
            first_valid = WINDOW - (t * tq + c * CHUNK)
            kidx = lax.broadcasted_iota(jnp.int32, s.shape, 2)
            s = jnp.where(kidx >= first_valid, s, NEG_INF)
        m = jnp.maximum(jnp.max(s, axis=-1, keepdims=True), sink)
        e = jnp.exp(s - m)
        den = jnp.sum(e, axis=-1, keepdims=True) + jnp.exp(sink - m)
        p = (e * (1.0 / den)).astype(BF16)
        o = jnp.einsum('hqk,hkd->hqd', p, v_scr[:, ks, :], preferred_element_type=F32)
        o = o.reshape(N_HEADS, CHUNK, HEAD_DIM).astype(o_ref.dtype)
        for h in range(N_HEADS):
            o_ref[cs, h * HEAD_DIM:(h + 1) * HEAD_DIM] = o[h]


def _attention(q, kv, k_prev, v_prev, bias, sinks, *, n_seq, seq, tq, row0, mask_first):
    nt = seq // tq
    qb0 = row0 // tq
    if mask_first:
        wpt = tq // WINDOW
        kp_spec = pl.BlockSpec((WINDOW, KV_DIM), lambda s, t: (jnp.maximum((s * nt + t) * wpt - 1, 0), 0))
        vp_spec = pl.BlockSpec((WINDOW, KV_DIM), lambda s, t: (jnp.maximum((s * nt + t) * wpt - 1, 0), 1))
    else:
        kp_spec = pl.BlockSpec((WINDOW, KV_DIM), lambda s, t: (s, 0))
        vp_spec = pl.BlockSpec((WINDOW, KV_DIM), lambda s, t: (s, 0))
    return pl.pallas_call(
        functools.partial(_attn_kernel, tq=tq, mask_first=mask_first),
        grid=(n_seq, nt),
        in_specs=[pl.BlockSpec((tq, D_MODEL), lambda s, t: (qb0 + s * nt + t, 0)),
                  kp_spec, vp_spec,
                  pl.BlockSpec((tq, KV_DIM), lambda s, t: (qb0 + s * nt + t, 0)),
                  pl.BlockSpec((tq, KV_DIM), lambda s, t: (qb0 + s * nt + t, 1)),
                  pl.BlockSpec((N_KV_HEADS, GROUP * CHUNK, BAND), lambda s, t: (0, 0, 0)),
                  pl.BlockSpec((N_KV_HEADS, GROUP * CHUNK, 1), lambda s, t: (0, 0, 0))],
        out_specs=pl.BlockSpec((tq, D_MODEL), lambda s, t: (s * nt + t, 0)),
        out_shape=jax.ShapeDtypeStruct((n_seq * seq, D_MODEL), BF16),
        scratch_shapes=[pltpu.VMEM((N_HEADS, tq, HEAD_DIM), BF16),
                        pltpu.VMEM((N_KV_HEADS, WINDOW + tq, HEAD_DIM), BF16),
                        pltpu.VMEM((N_KV_HEADS, WINDOW + tq, HEAD_DIM), BF16)],
        compiler_params=_params(2, 40),
    )(q, k_prev, v_prev, kv, kv, bias, sinks)


def _oproj_ln1_kernel(o_ref, x_ref, w_ref, g_ref, b_ref, h_ref):
    mix = _dot(o_ref[...], w_ref[...])
    h_ref[...] = _ln(ALPHA * x_ref[...] + mix, g_ref[...], b_ref[...])


def _oproj_ln1(o, x, w, g, b):
    n = o.shape[0]
    bm = 256
    row = pl.BlockSpec((bm, D_MODEL), lambda i: (i, 0))
    vec = pl.BlockSpec((1, D_MODEL), lambda i: (0, 0))
    return pl.pallas_call(
        _oproj_ln1_kernel,
        grid=(n // bm,),
        in_specs=[row, row, pl.BlockSpec((D_MODEL, D_MODEL), lambda i: (0, 0)), vec, vec],
        out_specs=row,
        out_shape=jax.ShapeDtypeStruct((n, D_MODEL), F32),
        compiler_params=_params(1, 48),
    )(o, x, w, g, b)


def _router_kernel(h_ref, wh_ref, wl_ref, b_ref, r_ref):
    h = h_ref[...]
    hh = h.astype(BF16)
    hl = (h - hh.astype(F32)).astype(BF16)
    logits = _dot(hh, wh_ref[...]) + (_dot(hl, wh_ref[...]) + _dot(hh, wl_ref[...])) + b_ref[...]
    lane = lax.broadcasted_iota(jnp.int32, logits.shape, 1).astype(F32)
    ninf = -jnp.inf
    lg = jnp.where(lane < N_EXPERTS, logits, ninf)
    m1 = jnp.max(lg, axis=-1, keepdims=True)
    i1 = jnp.min(jnp.where(lg == m1, lane, float(LANES)), axis=-1, keepdims=True)
    lg2 = jnp.where(lane == i1, ninf, lg)
    m2 = jnp.max(lg2, axis=-1, keepdims=True)
    i2 = jnp.min(jnp.where(lg2 == m2, lane, float(LANES)), axis=-1, keepdims=True)
    ex = jnp.exp(m2 - m1)
    den = 1.0 + ex
    out = jnp.where(lane == 0, i1, jnp.where(lane == 1, i2, jnp.where(lane == 2, 1.0 / den, ex / den)))
    r_ref[...] = jnp.where(lane < 4, out, 0.0)


def _router(h, w, b):
    n = h.shape[0]
    bm = math.gcd(n, 512)
    wp = jnp.zeros((D_MODEL, LANES), F32).at[:, :N_EXPERTS].set(w.astype(F32))
    wh = wp.astype(BF16)
    wl = (wp - wh.astype(F32)).astype(BF16)
    bp = jnp.zeros((1, LANES), F32).at[0, :N_EXPERTS].set(b.astype(F32))
    return pl.pallas_call(
        _router_kernel,
        grid=(n // bm,),
        in_specs=[pl.BlockSpec((bm, D_MODEL), lambda i: (i, 0)),
                  pl.BlockSpec((D_MODEL, LANES), lambda i: (0, 0)),
                  pl.BlockSpec((D_MODEL, LANES), lambda i: (0, 0)),
                  pl.BlockSpec((1, LANES), lambda i: (0, 0))],
        out_specs=pl.BlockSpec((bm, LANES), lambda i: (i, 0)),
        out_shape=jax.ShapeDtypeStruct((n, LANES), F32),
        compiler_params=_params(1, 32),
    )(h, wh, wl, bp)


def _gather_kernel(tok_ref, h_hbm, o_ref, buf, sem, *, rows, n_blocks):
    i = pl.program_id(0)

    def row_copy(tok, slot, r):
        return pltpu.make_async_copy(h_hbm.at[pl.ds(tok, 1)], buf.at[slot, pl.ds(r, 1)], sem.at[slot])

    def issue(blk, slot):
        def body(r, c):
            row_copy(tok_ref[blk * rows + r], slot, r).start()
            return c
        lax.fori_loop(0, rows, body, 0, unroll=DMA_ISSUE_UNROLL)

    @pl.when(i == 0)
    def _():
        issue(0, 0)

    @pl.when(i + 1 < n_blocks)
    def _():
        issue(i + 1, (i + 1) % 2)

    slot = i % 2
    pltpu.make_async_copy(h_hbm.at[pl.ds(0, rows)], buf.at[slot], sem.at[slot]).wait()
    o_ref[...] = buf[slot].astype(o_ref.dtype)


def _gather_rows(slot_tok, h):
    cap = slot_tok.shape[0]
    rows = MOE_ROWS
    n_blocks = cap // rows
    return pl.pallas_call(
        functools.partial(_gather_kernel, rows=rows, n_blocks=n_blocks),
        grid_spec=pltpu.PrefetchScalarGridSpec(
            num_scalar_prefetch=1,
            grid=(n_blocks,),
            in_specs=[pl.BlockSpec(memory_space=pl.ANY)],
            out_specs=pl.BlockSpec((rows, D_MODEL), lambda i, tok: (i, 0)),
            scratch_shapes=[pltpu.VMEM((2, rows, D_MODEL), F32), pltpu.SemaphoreType.DMA((2,))]),
        out_shape=jax.ShapeDtypeStruct((cap, D_MODEL), BF16),
        compiler_params=_params(1, 40),
    )(slot_tok, h)


def _expert_changed(be_ref, i):
    return jnp.logical_or(i == 0, be_ref[i] != be_ref[jnp.maximum(i - 1, 0)])


def _moe_up_kernel(be_ref, nu_ref, x_ref, wg_ref, wu_ref, a_ref, wgb_ref, wub_ref):
    i = pl.program_id(1)
    used = i < nu_ref[0]

    @pl.when(jnp.logical_and(used, _expert_changed(be_ref, i)))
    def _():
        wgb_ref[...] = wg_ref[...].astype(BF16)
        wub_ref[...] = wu_ref[...].astype(BF16)

    @pl.when(used)
    def _():
        x = x_ref[...]
        g = _dot(x, wgb_ref[...])
        u = _dot(x, wub_ref[...])
        a_ref[...] = (_silu(g) * u).astype(a_ref.dtype)

    @pl.when(jnp.logical_not(used))
    def _():
        a_ref[...] = jnp.zeros(a_ref.shape, a_ref.dtype)


def _moe_up(block_e, n_used, xs, wg, wu):
    cap = xs.shape[0]
    bf = 512
    return pl.pallas_call(
        _moe_up_kernel,
        grid_spec=pltpu.PrefetchScalarGridSpec(
            num_scalar_prefetch=2,
            grid=(D_FF // bf, cap // MOE_ROWS),
            in_specs=[pl.BlockSpec((MOE_ROWS, D_MODEL), lambda j, i, be, nu: (i, 0)),
                      pl.BlockSpec((None, D_MODEL, bf), lambda j, i, be, nu: (be[i], 0, j)),
                      pl.BlockSpec((None, D_MODEL, bf), lambda j, i, be, nu: (be[i], 0, j))],
            out_specs=pl.BlockSpec((MOE_ROWS, bf), lambda j, i, be, nu: (i, j)),
            scratch_shapes=[pltpu.VMEM((D_MODEL, bf), BF16), pltpu.VMEM((D_MODEL, bf), BF16)]),
        out_shape=jax.ShapeDtypeStruct((cap, D_FF), BF16),
        compiler_params=_params(2, 48),
    )(block_e, n_used, xs, wg, wu)


def _moe_down_kernel(be_ref, nu_ref, a_ref, wd_ref, y_ref, wdb_ref):
    i = pl.program_id(1)
    used = i < nu_ref[0]

    @pl.when(jnp.logical_and(used, _expert_changed(be_ref, i)))
    def _():
        wdb_ref[...] = wd_ref[...].astype(BF16)

    @pl.when(used)
    def _():
        y_ref[...] = _dot(a_ref[...], wdb_ref[...])

    @pl.when(jnp.logical_not(used))
    def _():
        y_ref[...] = jnp.zeros(y_ref.shape, y_ref.dtype)


def _moe_down(block_e, n_used, a, wd):
    cap = a.shape[0]
    bn = 512
    return pl.pallas_call(
        _moe_down_kernel,
        grid_spec=pltpu.PrefetchScalarGridSpec(
            num_scalar_prefetch=2,
            grid=(D_MODEL // bn, cap // MOE_ROWS),
            in_specs=[pl.BlockSpec((MOE_ROWS, D_FF), lambda j, i, be, nu: (i, 0)),
                      pl.BlockSpec((None, D_FF, bn), lambda j, i, be, nu: (be[i], 0, j))],
            out_specs=pl.BlockSpec((MOE_ROWS, bn), lambda j, i, be, nu: (i, j)),
            scratch_shapes=[pltpu.VMEM((D_FF, bn), BF16)]),
        out_shape=jax.ShapeDtypeStruct((cap, D_MODEL), F32),
        compiler_params=_params(2, 56),
    )(block_e, n_used, a, wd)


def _combine_kernel(dest_ref, h_ref, r_ref, p_ref, wg_ref, wp_ref, g_ref, b_ref, ys_hbm, y_ref, buf, sem,
                    *, bm, row0, n_blocks):
    i = pl.program_id(0)

    def row_copy(src, slot, k, r):
        return pltpu.make_async_copy(ys_hbm.at[pl.ds(src, 1)], buf.at[slot, k, pl.ds(r, 1)], sem.at[slot])

    def issue(blk, slot):
        def body(r, c):
            a = TOP_K * (row0 + blk * bm + r)
            for k in range(TOP_K):
                row_copy(dest_ref[a + k], slot, k, r).start()
            return c
        lax.fori_loop(0, bm, body, 0, unroll=DMA_ISSUE_UNROLL)

    @pl.when(i == 0)
    def _():
        issue(0, 0)

    @pl.when(i + 1 < n_blocks)
    def _():
        issue(i + 1, (i + 1) % 2)

    slot = i % 2
    for k in range(TOP_K):
        pltpu.make_async_copy(ys_hbm.at[pl.ds(0, bm)], buf.at[slot, k], sem.at[slot]).wait()

    h = h_ref[...]
    r = r_ref[...]
    f = buf[slot, 0] * r[:, 2:3] + buf[slot, 1] * r[:, 3:4]
    gate = jax.nn.sigmoid(_dot(h.astype(BF16), wg_ref[...]))
    proj = _dot(p_ref[...].astype(BF16), wp_ref[...])
    y_ref[...] = _ln(ALPHA * h + f + gate * proj, g_ref[...], b_ref[...])


def _combine_ple_ln2(dest, h, route, p, ys, wg, wp, g, b, *, row0, n_rows):
    bm = 256
    n_blocks = n_rows // bm
    b0 = row0 // bm
    vec = pl.BlockSpec((1, D_MODEL), lambda i, d: (0, 0))
    return pl.pallas_call(
        functools.partial(_combine_kernel, bm=bm, row0=row0, n_blocks=n_blocks),
        grid_spec=pltpu.PrefetchScalarGridSpec(
            num_scalar_prefetch=1,
            grid=(n_blocks,),
            in_specs=[pl.BlockSpec((bm, D_MODEL), lambda i, d: (b0 + i, 0)),
                      pl.BlockSpec((bm, LANES), lambda i, d: (b0 + i, 0)),
                      pl.BlockSpec((bm, PLE_DIM), lambda i, d: (b0 + i, 0)),
                      pl.BlockSpec((D_MODEL, D_MODEL), lambda i, d: (0, 0)),
                      pl.BlockSpec((PLE_DIM, D_MODEL), lambda i, d: (0, 0)),
                      vec, vec,
                      pl.BlockSpec(memory_space=pl.ANY)],
            out_specs=pl.BlockSpec((bm, D_MODEL), lambda i, d: (i, 0)),
            scratch_shapes=[pltpu.VMEM((2, TOP_K, bm, D_MODEL), F32), pltpu.SemaphoreType.DMA((2,))]),
        out_shape=jax.ShapeDtypeStruct((n_rows, D_MODEL), F32),
        compiler_params=_params(1, 48),
    )(dest, h, route, p, wg, wp, g, b, ys)


def _route_plan(route):
    n = route.shape[0]
    n_assign = n * TOP_K
    flat_e = route[:, :TOP_K].astype(jnp.int32).reshape(n_assign)
    onehot = (flat_e[:, None] == jnp.arange(N_EXPERTS, dtype=jnp.int32)[None, :]).astype(jnp.int32)
    csum = jnp.cumsum(onehot, axis=0)
    counts = csum[-1]
    rank = jnp.sum(csum * onehot, axis=1) - 1
    padded = (counts + MOE_ROWS - 1) // MOE_ROWS * MOE_ROWS
    pad_end = jnp.cumsum(padded)
    dest = ((pad_end - padded)[flat_e] + rank).astype(jnp.int32)
    n_blocks = -(-n_assign // MOE_ROWS) + N_EXPERTS - 1
    cap = n_blocks * MOE_ROWS
    slot_tok = jnp.zeros((cap,), jnp.int32).at[dest].set(jnp.arange(n_assign, dtype=jnp.int32) // TOP_K)
    block_e = jnp.minimum(jnp.searchsorted(pad_end, jnp.arange(n_blocks) * MOE_ROWS, side='right'),
                          N_EXPERTS - 1).astype(jnp.int32)
    n_used = (pad_end[-1:] // MOE_ROWS).astype(jnp.int32)
    return dest, slot_tok, block_e, n_used


def kernel(x_prompt, x_sample, state_conv, cache_k, cache_v, p_prompt, p_sample, conv_w_pw1, conv_b_pw1, conv_w_dw, conv_b_dw, conv_ln_g, conv_ln_b, conv_w_pw2, conv_b_pw2, w_kv, rel_bias, attn_w_q, attn_sinks, attn_w_o, ln1_g, ln1_b, ln2_g, ln2_b, ffn_w_gate, ffn_w_up, ffn_w_down, moe_w_router, moe_b_router, moe_w_gate, moe_w_up, moe_w_down, ple_w_gate, ple_w_proj):
    n_pb, p_seq, _ = x_prompt.shape
    n_sb, s_seq, _ = x_sample.shape
    n_p, n_s = n_pb * p_seq, n_sb * s_seq
    vec = lambda v: v.reshape(1, -1).astype(F32)

    xp = x_prompt.reshape(n_p, D_MODEL)
    xs = x_sample.reshape(n_s, D_MODEL)
    w1 = conv_w_pw1[0].astype(BF16)
    b1 = vec(conv_b_pw1[0])
    u_p = _pw1_glu(xp, w1, b1)
    u_s = _pw1_glu(xs, w1, b1)
    w_dw, b_dw = conv_w_dw[0].astype(F32), vec(conv_b_dw[0])
    state = jnp.pad(state_conv[0], ((0, 0), (CONV_HALO - (CONV_WIDTH - 1), 0), (0, 0))).reshape(n_sb * CONV_HALO, D_MODEL)
    yc_p = _dwconv(u_p, u_p, w_dw, b_dw, n_seq=n_pb, seq=p_seq, bt=256, zero_first=True)
    yc_s = _dwconv(u_s, state, w_dw, b_dw, n_seq=n_sb, seq=s_seq, bt=s_seq, zero_first=False)
    w2 = conv_w_pw2[0].astype(BF16)
    tail0 = (w2, vec(conv_b_pw2[0]), vec(conv_ln_g[0]), vec(conv_ln_b[0]), vec(ln1_g[0]), vec(ln1_b[0]))
    h_p, hb_p = _pw2_ln1(yc_p, xp, *tail0)
    h_s, hb_s = _pw2_ln1(yc_s, xs, *tail0)
    h = jnp.concatenate([h_p, h_s], axis=0)
    hb = jnp.concatenate([hb_p, hb_s], axis=0)
    f = _ffn(hb, ffn_w_gate[0].astype(BF16), ffn_w_up[0].astype(BF16), ffn_w_down[0].astype(BF16))
    p0 = jnp.concatenate([p_prompt[0].reshape(n_p, PLE_DIM), p_sample[0].reshape(n_s, PLE_DIM)], axis=0)
    x1, x1b = _ple_ln2(h, hb, f, p0, ple_w_gate[0].astype(BF16), ple_w_proj[0].astype(BF16),
                       vec(ln2_g[0]), vec(ln2_b[0]))

    q = _mm(x1b, attn_w_q[0].astype(BF16), scale=HEAD_DIM ** -0.5, out_dtype=BF16)
    kv = _mm(x1b, w_kv.astype(BF16), scale=1.0, out_dtype=F32)
    bias = _rel_bias(rel_bias)
    sinks = jnp.repeat(attn_sinks[0].astype(F32), CHUNK).reshape(N_KV_HEADS, GROUP * CHUNK, 1)
    o_p = _attention(q, kv, kv, kv, bias, sinks, n_seq=n_pb, seq=p_seq, tq=256, row0=0, mask_first=True)
    o_s = _attention(q, kv, cache_k.reshape(n_sb * WINDOW, KV_DIM), cache_v.reshape(n_sb * WINDOW, KV_DIM),
                     bias, sinks, n_seq=n_sb, seq=s_seq, tq=s_seq, row0=n_p, mask_first=False)
    o = jnp.concatenate([o_p, o_s], axis=0)
    h1 = _oproj_ln1(o, x1, attn_w_o[0].astype(BF16), vec(ln1_g[1]), vec(ln1_b[1]))

    route = _router(h1, moe_w_router[0], moe_b_router[0])
    dest, slot_tok, block_e, n_used = _route_plan(route)
    xg = _gather_rows(slot_tok, h1)
    a = _moe_up(block_e, n_used, xg, moe_w_gate[0], moe_w_up[0])
    ys = _moe_down(block_e, n_used, a, moe_w_down[0])
    p1 = jnp.concatenate([p_prompt[1].reshape(n_p, PLE_DIM), p_sample[1].reshape(n_s, PLE_DIM)], axis=0)
    tail1 = (ple_w_gate[1].astype(BF16), ple_w_proj[1].astype(BF16), vec(ln2_g[1]), vec(ln2_b[1]))
    y_p = _combine_ple_ln2(dest, h1, route, p1, ys, *tail1, row0=0, n_rows=n_p)
    y_s = _combine_ple_ln2(dest, h1, route, p1, ys, *tail1, row0=n_p, n_rows=n_s)

    keep = CONV_WIDTH - 1
    conv_p = u_p.reshape(n_pb, p_seq, D_MODEL)[None, :, p_seq - keep:]
    conv_s = u_s.reshape(n_sb, s_seq, D_MODEL)[None, :, s_seq - keep:]
    kv_p = kv[:n_p].reshape(n_pb, p_seq, 2, N_KV_HEADS, HEAD_DIM)[:, p_seq - WINDOW:]
    kv_s = kv[n_p:].reshape(n_sb, s_seq, 2, N_KV_HEADS, HEAD_DIM)
    k_s = jnp.concatenate([cache_k, kv_s[:, :, 0]], axis=1)[:, s_seq:]
    v_s = jnp.concatenate([cache_v, kv_s[:, :, 1]], axis=1)[:, s_seq:]
    return (y_p.reshape(n_pb, p_seq, D_MODEL), y_s.reshape(n_sb, s_seq, D_MODEL), conv_p, conv_s,
            kv_p[:, :, 0], kv_p[:, :, 1], k_s, v_s)
```

```python
import functools
import math

import jax
import jax.numpy as jnp
from jax import lax
from jax.experimental import pallas as pl
from jax.experimental.pallas import tpu as pltpu

D_MODEL = 2048
CHUNK = 64
CONV_WIDTH = 31
HEAD_DIM = 64
N_HEADS = 32
N_KV_HEADS = 8
GROUP = N_HEADS // N_KV_HEADS
WINDOW = 128
BAND = WINDOW + CHUNK
NUM_BUCKETS = 32
MAX_DISTANCE = 128
D_FF = 5632
N_EXPERTS = 8
TOP_K = 2
PLE_DIM = 256
DEPTH = 2
ALPHA = (2 * DEPTH) ** 0.25
LN_EPS = 1e-5
NEG_INF = -1e30

KV_DIM = N_KV_HEADS * HEAD_DIM
CONV_HALO = 32
MOE_ROWS = 512
DMA_ISSUE_UNROLL = 8
DMA_PRIORITIES = 2
LANES = 128
SUBLANES = 8
MIB = 1024 * 1024

F32 = jnp.float32
BF16 = jnp.bfloat16


def _params(n_axes, vmem_mib):
    return pltpu.CompilerParams(dimension_semantics=("arbitrary",) * n_axes,
                                vmem_limit_bytes=vmem_mib * MIB)


def _dot(a, b):
    return jnp.dot(a, b, preferred_element_type=F32)


def _ln(x, g, b):
    mu = jnp.mean(x, axis=-1, keepdims=True)
    xc = x - mu
    var = jnp.mean(xc * xc, axis=-1, keepdims=True)
    return xc * lax.rsqrt(var + LN_EPS) * g + b


def _silu(x):
    return x * jax.nn.sigmoid(x)


def _pw1_glu_kernel(x_ref, wa_ref, wb_ref, ba_ref, bb_ref, u_ref, xb_ref):
    @pl.when(pl.program_id(1) == 0)
    def _():
        xb_ref[...] = x_ref[...].astype(BF16)

    xb = xb_ref[...]
    a = _dot(xb, wa_ref[...]) + ba_ref[...]
    g = _dot(xb, wb_ref[...]) + bb_ref[...]
    u_ref[...] = a * jax.nn.sigmoid(g)


def _pw1_glu(x, w, b):
    n = x.shape[0]
    bm, bn = math.gcd(n, 1024), 512
    nj = D_MODEL // bn
    return pl.pallas_call(
        _pw1_glu_kernel,
        grid=(n // bm, nj),
        in_specs=[pl.BlockSpec((bm, D_MODEL), lambda i, j: (i, 0)),
                  pl.BlockSpec((D_MODEL, bn), lambda i, j: (0, j)),
                  pl.BlockSpec((D_MODEL, bn), lambda i, j: (0, j + nj)),
                  pl.BlockSpec((1, bn), lambda i, j: (0, j)),
                  pl.BlockSpec((1, bn), lambda i, j: (0, j + nj))],
        out_specs=pl.BlockSpec((bm, bn), lambda i, j: (i, j)),
        out_shape=jax.ShapeDtypeStruct((n, D_MODEL), F32),
        scratch_shapes=[pltpu.VMEM((bm, D_MODEL), BF16)],
        compiler_params=_params(2, 48),
    )(x, w, w, b, b)


def _dwconv_kernel(prev_ref, cur_ref, w_ref, b_ref, y_ref, e_ref, *, bt, zero_first, rc):
    prev = prev_ref[...]
    if zero_first:
        prev = jnp.where(pl.program_id(1) == 0, 0.0, prev)
    e_ref[0, 0:CONV_HALO, :] = prev
    e_ref[0, CONV_HALO:, :] = cur_ref[...]
    e0 = e_ref[0]
    n_e = bt + CONV_HALO
    for rho in range(1, SUBLANES):
        e_ref[rho] = pltpu.roll(e0, n_e - rho, axis=0)
    off = CONV_HALO - (CONV_WIDTH - 1)
    for r0 in range(0, bt, rc):
        acc = jnp.broadcast_to(b_ref[...], (rc, b_ref.shape[1]))
        for k in range(CONV_WIDTH):
            rho = (k + off) % SUBLANES
            base = r0 + k + off - rho
            acc = acc + e_ref[rho, base:base + rc, :] * w_ref[k:k + 1, :]
        y_ref[r0:r0 + rc, :] = acc


def _dwconv(u, prev_src, w, b, *, n_seq, seq, bt, zero_first):
    bc = 512
    nt = seq // bt
    hb = bt // CONV_HALO
    if zero_first:
        prev_map = lambda s, t, c: (jnp.maximum((s * nt + t) * hb - 1, 0), c)
    else:
        prev_map = lambda s, t, c: (s, c)
    return pl.pallas_call(
        functools.partial(_dwconv_kernel, bt=bt, zero_first=zero_first, rc=32),
        grid=(n_seq, nt, D_MODEL // bc),
        in_specs=[pl.BlockSpec((CONV_HALO, bc), prev_map),
                  pl.BlockSpec((bt, bc), lambda s, t, c: (s * nt + t, c)),
                  pl.BlockSpec((CONV_WIDTH, bc), lambda s, t, c: (0, c)),
                  pl.BlockSpec((1, bc), lambda s, t, c: (0, c))],
        out_specs=pl.BlockSpec((bt, bc), lambda s, t, c: (s * nt + t, c)),
        out_shape=jax.ShapeDtypeStruct(u.shape, F32),
        scratch_shapes=[pltpu.VMEM((SUBLANES, bt + CONV_HALO, bc), F32)],
        compiler_params=_params(3, 32),
    )(prev_src, u, w, b)


def _two_group_specs(bm, width, n_first):
    nb = n_first // bm
    return (pl.BlockSpec((bm, width), lambda i, *_: (jnp.minimum(i, nb - 1), 0)),
            pl.BlockSpec((bm, width), lambda i, *_: (jnp.maximum(i - nb, 0), 0)))


def _pick_group(first_ref, second_ref, n_first_blocks):
    return jnp.where(pl.program_id(0) < n_first_blocks, first_ref[...], second_ref[...])


def _pw2_ln1_kernel(yp_ref, ys_ref, xp_ref, xs_ref, w_ref, b2_ref, cg_ref, cb_ref, g_ref, b_ref, h_ref, hb_ref,
                    *, nb_first):
    y = _silu(_ln(_pick_group(yp_ref, ys_ref, nb_first), cg_ref[...], cb_ref[...]))
    mix = _dot(y.astype(BF16), w_ref[...]) + b2_ref[...]
    h = _ln(ALPHA * _pick_group(xp_ref, xs_ref, nb_first) + mix, g_ref[...], b_ref[...])
    h_ref[...] = h
    hb_ref[...] = h.astype(BF16)


def _pw2_ln1(y_p, y_s, x_p, x_s, w, b2, cg, cb, g, b):
    n_p, n = y_p.shape[0], y_p.shape[0] + y_s.shape[0]
    bm = 256
    row = pl.BlockSpec((bm, D_MODEL), lambda i: (i, 0))
    vec = pl.BlockSpec((1, D_MODEL), lambda i: (0, 0))
    return pl.pallas_call(
        functools.partial(_pw2_ln1_kernel, nb_first=n_p // bm),
        grid=(n // bm,),
        in_specs=[*_two_group_specs(bm, D_MODEL, n_p), *_two_group_specs(bm, D_MODEL, n_p),
                  pl.BlockSpec((D_MODEL, D_MODEL), lambda i: (0, 0)), vec, vec, vec, vec, vec],
        out_specs=[row, row],
        out_shape=[jax.ShapeDtypeStruct((n, D_MODEL), F32), jax.ShapeDtypeStruct((n, D_MODEL), BF16)],
        compiler_params=_params(1, 56),
    )(y_p, y_s, x_p, x_s, w, b2, cg, cb, g, b)


def _ffn_kernel(x_ref, wg_ref, wu_ref, wd_ref, o_ref):
    j = pl.program_id(1)
    x = x_ref[...]
    g = _dot(x, wg_ref[...])
    u = _dot(x, wu_ref[...])
    part = _dot((_silu(g) * u).astype(BF16), wd_ref[...])

    @pl.when(j == 0)
    def _():
        o_ref[...] = part

    @pl.when(j > 0)
    def _():
        o_ref[...] += part


def _ffn(xb, wg, wu, wd):
    n = xb.shape[0]
    bm, bf = math.gcd(n, 1024), 512
    return pl.pallas_call(
        _ffn_kernel,
        grid=(n // bm, D_FF // bf),
        in_specs=[pl.BlockSpec((bm, D_MODEL), lambda i, j: (i, 0)),
                  pl.BlockSpec((D_MODEL, bf), lambda i, j: (0, j)),
                  pl.BlockSpec((D_MODEL, bf), lambda i, j: (0, j)),
                  pl.BlockSpec((bf, D_MODEL), lambda i, j: (j, 0))],
        out_specs=pl.BlockSpec((bm, D_MODEL), lambda i, j: (i, 0)),
        out_shape=jax.ShapeDtypeStruct((n, D_MODEL), F32),
        compiler_params=_params(2, 60),
    )(xb, wg, wu, wd)


def _ple_ln2_kernel(h_ref, hb_ref, f_ref, pp_ref, ps_ref, wg_ref, wp_ref, g_ref, b_ref, x_ref, xb_ref, *, nb_first):
    gate = jax.nn.sigmoid(_dot(hb_ref[...], wg_ref[...]))
    proj = _dot(_pick_group(pp_ref, ps_ref, nb_first).astype(BF16), wp_ref[...])
    x = _ln(ALPHA * h_ref[...] + f_ref[...] + gate * proj, g_ref[...], b_ref[...])
    x_ref[...] = x
    xb_ref[...] = x.astype(BF16)


def _ple_ln2(h, hb, f, p_p, p_s, wg, wp, g, b):
    n = h.shape[0]
    bm = 256
    row = pl.BlockSpec((bm, D_MODEL), lambda i: (i, 0))
    vec = pl.BlockSpec((1, D_MODEL), lambda i: (0, 0))
    return pl.pallas_call(
        functools.partial(_ple_ln2_kernel, nb_first=p_p.shape[0] // bm),
        grid=(n // bm,),
        in_specs=[row, row, row, *_two_group_specs(bm, PLE_DIM, p_p.shape[0]),
                  pl.BlockSpec((D_MODEL, D_MODEL), lambda i: (0, 0)),
                  pl.BlockSpec((PLE_DIM, D_MODEL), lambda i: (0, 0)), vec, vec],
        out_specs=[row, row],
        out_shape=[jax.ShapeDtypeStruct((n, D_MODEL), F32), jax.ShapeDtypeStruct((n, D_MODEL), BF16)],
        compiler_params=_params(1, 48),
    )(h, hb, f, p_p, p_s, wg, wp, g, b)


def _mm_kernel(x_ref, w_ref, o_ref, *, scale):
    acc = _dot(x_ref[...], w_ref[...])
    if scale != 1.0:
        acc = acc * scale
    o_ref[...] = acc.astype(o_ref.dtype)


def _mm(xb, w, *, scale, out_dtype):
    n, k = xb.shape
    m = w.shape[1]
    bm, bn = math.gcd(n, 1024), 512
    return pl.pallas_call(
        functools.partial(_mm_kernel, scale=scale),
        grid=(n // bm, m // bn),
        in_specs=[pl.BlockSpec((bm, k), lambda i, j: (i, 0)),
                  pl.BlockSpec((k, bn), lambda i, j: (0, j))],
        out_specs=pl.BlockSpec((bm, bn), lambda i, j: (i, j)),
        out_shape=jax.ShapeDtypeStruct((n, m), out_dtype),
        compiler_params=_params(2, 32),
    )(xb, w)


def _t5_bucket(rel):
    half = NUM_BUCKETS // 2
    exact = half // 2
    ret = jnp.where(rel > 0, half, 0)
    n = jnp.abs(rel)
    large = exact + (jnp.log(jnp.maximum(n, 1).astype(F32) / exact)
                     / math.log(MAX_DISTANCE / exact) * (half - exact)).astype(jnp.int32)
    large = jnp.minimum(large, half - 1)
    return ret + jnp.where(n < exact, n, large)


def _bias_kernel(table_ref, bucket_ref, o_ref):
    bk = bucket_ref[...]
    masks = [bk == b for b in range(NUM_BUCKETS)]
    for h in range(N_HEADS):
        acc = jnp.zeros(bk.shape, F32)
        for b in range(NUM_BUCKETS):
            acc = jnp.where(masks[b], table_ref[b, h], acc)
        o_ref[h] = acc


def _rel_bias(table):
    i = jnp.arange(CHUNK)[:, None]
    j = jnp.arange(BAND)[None, :]
    bucket = _t5_bucket((j - WINDOW) - i).astype(jnp.int32)
    out = pl.pallas_call(
        _bias_kernel,
        in_specs=[pl.BlockSpec(memory_space=pltpu.SMEM), pl.BlockSpec(memory_space=pltpu.VMEM)],
        out_specs=pl.BlockSpec(memory_space=pltpu.VMEM),
        out_shape=jax.ShapeDtypeStruct((N_HEADS, CHUNK, BAND), F32),
    )(table.astype(F32), bucket)
    return out.reshape(N_KV_HEADS, GROUP * CHUNK, BAND)


def _attn_kernel(q_ref, kp_ref, vp_ref, kc_ref, vc_ref, bias_ref, sink_ref, o_ref, q_scr, k_scr, v_scr,
                 *, tq, mask_first):
    t = pl.program_id(1)
    for h in range(N_KV_HEADS):
        hs = slice(h * HEAD_DIM, (h + 1) * HEAD_DIM)
        k_scr[h, 0:WINDOW, :] = kp_ref[:, hs].astype(BF16)
        k_scr[h, WINDOW:, :] = kc_ref[:, hs].astype(BF16)
        v_scr[h, 0:WINDOW, :] = vp_ref[:, hs].astype(BF16)
        v_scr[h, WINDOW:, :] = vc_ref[:, hs].astype(BF16)
    for h in range(N_HEADS):
        q_scr[h] = q_ref[:, h * HEAD_DIM:(h + 1) * HEAD_DIM]
    rows = GROUP * CHUNK
    bias = bias_ref[...]
    sink = sink_ref[...]
    for c in range(tq // CHUNK):
        cs = slice(c * CHUNK, (c + 1) * CHUNK)
        ks = slice(c * CHUNK, c * CHUNK + BAND)
        qh = q_scr[:, cs, :].reshape(N_KV_HEADS, rows, HEAD_DIM)
        s = jnp.einsum('hqd,hkd->hqk', qh, k_scr[:, ks, :], preferred_element_type=F32) + bias
        if mask_first:
            first_valid = WINDOW - (t * tq + c * CHUNK)
            kidx = lax.broadcasted_iota(jnp.int32, s.shape, 2)
            s = jnp.where(kidx >= first_valid, s, NEG_INF)
        m = jnp.maximum(jnp.max(s, axis=-1, keepdims=True), sink)
        e = jnp.exp(s - m)
        den = jnp.sum(e, axis=-1, keepdims=True) + jnp.exp(sink - m)
        p = (e * (1.0 / den)).astype(BF16)
        o = jnp.einsum('hqk,hkd->hqd', p, v_scr[:, ks, :], preferred_element_type=F32)
        o = o.reshape(N_HEADS, CHUNK, HEAD_DIM).astype(o_ref.dtype)
        for h in range(N_HEADS):
            o_ref[cs, h * HEAD_DIM:(h + 1) * HEAD_DIM] = o[h]


def _attention(q, kv, k_prev, v_prev, bias, sinks, *, n_seq, seq, tq, row0, mask_first):
    nt = seq // tq
    qb0 = row0 // tq
    if mask_first:
        wpt = tq // WINDOW
        kp_spec = pl.BlockSpec((WINDOW, KV_DIM), lambda s, t: (jnp.maximum((s * nt + t) * wpt - 1, 0), 0))
        vp_spec = pl.BlockSpec((WINDOW, KV_DIM), lambda s, t: (jnp.maximum((s * nt + t) * wpt - 1, 0), 1))
    else:
        kp_spec = pl.BlockSpec((WINDOW, KV_DIM), lambda s, t: (s, 0))
        vp_spec = pl.BlockSpec((WINDOW, KV_DIM), lambda s, t: (s, 0))
    return pl.pallas_call(
        functools.partial(_attn_kernel, tq=tq, mask_first=mask_first),
        grid=(n_seq, nt),
        in_specs=[pl.BlockSpec((tq, D_MODEL), lambda s, t: (qb0 + s * nt + t, 0)),
                  kp_spec, vp_spec,
                  pl.BlockSpec((tq, KV_DIM), lambda s, t: (qb0 + s * nt + t, 0)),
                  pl.BlockSpec((tq, KV_DIM), lambda s, t: (qb0 + s * nt + t, 1)),
                  pl.BlockSpec((N_KV_HEADS, GROUP * CHUNK, BAND), lambda s, t: (0, 0, 0)),
                  pl.BlockSpec((N_KV_HEADS, GROUP * CHUNK, 1), lambda s, t: (0, 0, 0))],
        out_specs=pl.BlockSpec((tq, D_MODEL), lambda s, t: (s * nt + t, 0)),
        out_shape=jax.ShapeDtypeStruct((n_seq * seq, D_MODEL), BF16),
        scratch_shapes=[pltpu.VMEM((N_HEADS, tq, HEAD_DIM), BF16),
                        pltpu.VMEM((N_KV_HEADS, WINDOW + tq, HEAD_DIM), BF16),
                        pltpu.VMEM((N_KV_HEADS, WINDOW + tq, HEAD_DIM), BF16)],
        compiler_params=_params(2, 40),
    )(q, k_prev, v_prev, kv, kv, bias, sinks)


def _f32_bits(x):
    return lax.bitcast_convert_type(x, jnp.uint32)


def _pack_bf16_pairs(x):
    c = x.shape[1] // 2
    lo = _f32_bits(x[:, :c].astype(BF16).astype(F32))
    hi = _f32_bits(x[:, c:].astype(BF16).astype(F32))
    return hi | (lo >> 16)


def _unpack_bf16_pairs(w):
    lo = lax.bitcast_convert_type(w << 16, F32).astype(BF16)
    hi = lax.bitcast_convert_type(w & jnp.uint32(0xFFFF0000), F32).astype(BF16)
    return lo, hi


def _oproj_ln1_kernel(op_ref, os_ref, x_ref, w_ref, g_ref, b_ref, h_ref, hp_ref, *, nb_first):
    mix = _dot(_pick_group(op_ref, os_ref, nb_first), w_ref[...])
    h = _ln(ALPHA * x_ref[...] + mix, g_ref[...], b_ref[...])
    h_ref[...] = h
    hp_ref[...] = _pack_bf16_pairs(h)


def _oproj_ln1(o_p, o_s, x, w, g, b):
    n = x.shape[0]
    bm = 256
    row = pl.BlockSpec((bm, D_MODEL), lambda i: (i, 0))
    vec = pl.BlockSpec((1, D_MODEL), lambda i: (0, 0))
    return pl.pallas_call(
        functools.partial(_oproj_ln1_kernel, nb_first=o_p.shape[0] // bm),
        grid=(n // bm,),
        in_specs=[*_two_group_specs(bm, D_MODEL, o_p.shape[0]), row,
                  pl.BlockSpec((D_MODEL, D_MODEL), lambda i: (0, 0)), vec, vec],
        out_specs=[row, pl.BlockSpec((bm, D_MODEL // 2), lambda i: (i, 0))],
        out_shape=[jax.ShapeDtypeStruct((n, D_MODEL), F32), jax.ShapeDtypeStruct((n, D_MODEL // 2), jnp.uint32)],
        compiler_params=_params(1, 48),
    )(o_p, o_s, x, w, g, b)


def _router_kernel(h_ref, wh_ref, wl_ref, b_ref, r_ref):
    h = h_ref[...]
    hh = h.astype(BF16)
    hl = (h - hh.astype(F32)).astype(BF16)
    logits = _dot(hh, wh_ref[...]) + (_dot(hl, wh_ref[...]) + _dot(hh, wl_ref[...])) + b_ref[...]
    lane = lax.broadcasted_iota(jnp.int32, logits.shape, 1).astype(F32)
    ninf = -jnp.inf
    lg = jnp.where(lane < N_EXPERTS, logits, ninf)
    m1 = jnp.max(lg, axis=-1, keepdims=True)
    i1 = jnp.min(jnp.where(lg == m1, lane, float(LANES)), axis=-1, keepdims=True)
    lg2 = jnp.where(lane == i1, ninf, lg)
    m2 = jnp.max(lg2, axis=-1, keepdims=True)
    i2 = jnp.min(jnp.where(lg2 == m2, lane, float(LANES)), axis=-1, keepdims=True)
    ex = jnp.exp(m2 - m1)
    den = 1.0 + ex
    out = jnp.where(lane == 0, i1, jnp.where(lane == 1, i2, jnp.where(lane == 2, 1.0 / den, ex / den)))
    r_ref[...] = jnp.where(lane < 4, out, 0.0)


def _router(h, w, b):
    n = h.shape[0]
    bm = math.gcd(n, 512)
    wp = jnp.zeros((D_MODEL, LANES), F32).at[:, :N_EXPERTS].set(w.astype(F32))
    wh = wp.astype(BF16)
    wl = (wp - wh.astype(F32)).astype(BF16)
    bp = jnp.zeros((1, LANES), F32).at[0, :N_EXPERTS].set(b.astype(F32))
    return pl.pallas_call(
        _router_kernel,
        grid=(n // bm,),
        in_specs=[pl.BlockSpec((bm, D_MODEL), lambda i: (i, 0)),
                  pl.BlockSpec((D_MODEL, LANES), lambda i: (0, 0)),
                  pl.BlockSpec((D_MODEL, LANES), lambda i: (0, 0)),
                  pl.BlockSpec((1, LANES), lambda i: (0, 0))],
        out_specs=pl.BlockSpec((bm, LANES), lambda i: (i, 0)),
        out_shape=jax.ShapeDtypeStruct((n, LANES), F32),
        compiler_params=_params(1, 32),
    )(h, wh, wl, bp)


def _gather_kernel(tok_ref, h_hbm, o_ref, buf, sem, *, rows, n_blocks):
    i = pl.program_id(0)

    def row_copy(tok, slot, r):
        return pltpu.make_async_copy(h_hbm.at[pl.ds(tok, 1)], buf.at[slot, pl.ds(r, 1)], sem.at[slot])

    def issue(blk, slot):
        def body(r8, c):
            for q in range(DMA_ISSUE_UNROLL):
                r = r8 * DMA_ISSUE_UNROLL + q
                row_copy(tok_ref[blk * rows + r], slot, r).start(priority=q % DMA_PRIORITIES)
            return c
        lax.fori_loop(0, rows // DMA_ISSUE_UNROLL, body, 0)

    @pl.when(i == 0)
    def _():
        issue(0, 0)

    @pl.when(i + 1 < n_blocks)
    def _():
        issue(i + 1, (i + 1) % 2)

    slot = i % 2
    pltpu.make_async_copy(h_hbm.at[pl.ds(0, rows)], buf.at[slot], sem.at[slot]).wait()
    lo, hi = _unpack_bf16_pairs(buf[slot])
    half = lo.shape[1]
    o_ref[:, :half] = lo
    o_ref[:, half:] = hi


def _gather_rows(slot_tok, hp):
    cap = slot_tok.shape[0]
    rows = MOE_ROWS
    n_blocks = cap // rows
    return pl.pallas_call(
        functools.partial(_gather_kernel, rows=rows, n_blocks=n_blocks),
        grid_spec=pltpu.PrefetchScalarGridSpec(
            num_scalar_prefetch=1,
            grid=(n_blocks,),
            in_specs=[pl.BlockSpec(memory_space=pl.ANY)],
            out_specs=pl.BlockSpec((rows, D_MODEL), lambda i, tok: (i, 0)),
            scratch_shapes=[pltpu.VMEM((2, rows, D_MODEL // 2), jnp.uint32), pltpu.SemaphoreType.DMA((2,))]),
        out_shape=jax.ShapeDtypeStruct((cap, D_MODEL), BF16),
        compiler_params=_params(1, 32),
    )(slot_tok, hp)


def _expert_changed(be_ref, i):
    return jnp.logical_or(i == 0, be_ref[i] != be_ref[jnp.maximum(i - 1, 0)])


def _moe_up_kernel(be_ref, nu_ref, x_ref, wg_ref, wu_ref, a_ref, wgb_ref, wub_ref):
    i = pl.program_id(1)
    used = i < nu_ref[0]

    @pl.when(jnp.logical_and(used, _expert_changed(be_ref, i)))
    def _():
        wgb_ref[...] = wg_ref[...].astype(BF16)
        wub_ref[...] = wu_ref[...].astype(BF16)

    @pl.when(used)
    def _():
        x = x_ref[...]
        g = _dot(x, wgb_ref[...])
        u = _dot(x, wub_ref[...])
        a_ref[...] = (_silu(g) * u).astype(a_ref.dtype)

    @pl.when(jnp.logical_not(used))
    def _():
        a_ref[...] = jnp.zeros(a_ref.shape, a_ref.dtype)


def _moe_up(block_e, n_used, xs, wg, wu):
    cap = xs.shape[0]
    bf = 512
    return pl.pallas_call(
        _moe_up_kernel,
        grid_spec=pltpu.PrefetchScalarGridSpec(
            num_scalar_prefetch=2,
            grid=(D_FF // bf, cap // MOE_ROWS),
            in_specs=[pl.BlockSpec((MOE_ROWS, D_MODEL), lambda j, i, be, nu: (i, 0)),
                      pl.BlockSpec((None, D_MODEL, bf), lambda j, i, be, nu: (be[i], 0, j)),
                      pl.BlockSpec((None, D_MODEL, bf), lambda j, i, be, nu: (be[i], 0, j))],
            out_specs=pl.BlockSpec((MOE_ROWS, bf), lambda j, i, be, nu: (i, j)),
            scratch_shapes=[pltpu.VMEM((D_MODEL, bf), BF16), pltpu.VMEM((D_MODEL, bf), BF16)]),
        out_shape=jax.ShapeDtypeStruct((cap, D_FF), BF16),
        compiler_params=_params(2, 48),
    )(block_e, n_used, xs, wg, wu)


def _moe_down_kernel(be_ref, nu_ref, a_ref, wd_ref, y_ref, wdb_ref):
    i = pl.program_id(1)
    used = i < nu_ref[0]

    @pl.when(jnp.logical_and(used, _expert_changed(be_ref, i)))
    def _():
        wdb_ref[...] = wd_ref[...].astype(BF16)

    @pl.when(used)
    def _():
        y_ref[...] = _dot(a_ref[...], wdb_ref[...])

    @pl.when(jnp.logical_not(used))
    def _():
        y_ref[...] = jnp.zeros(y_ref.shape, y_ref.dtype)


def _moe_down(block_e, n_used, a, wd):
    cap = a.shape[0]
    bn = 512
    return pl.pallas_call(
        _moe_down_kernel,
        grid_spec=pltpu.PrefetchScalarGridSpec(
            num_scalar_prefetch=2,
            grid=(D_MODEL // bn, cap // MOE_ROWS),
            in_specs=[pl.BlockSpec((MOE_ROWS, D_FF), lambda j, i, be, nu: (i, 0)),
                      pl.BlockSpec((None, D_FF, bn), lambda j, i, be, nu: (be[i], 0, j))],
            out_specs=pl.BlockSpec((MOE_ROWS, bn), lambda j, i, be, nu: (i, j)),
            scratch_shapes=[pltpu.VMEM((D_FF, bn), BF16)]),
        out_shape=jax.ShapeDtypeStruct((cap, D_MODEL), F32),
        compiler_params=_params(2, 56),
    )(block_e, n_used, a, wd)


def _combine_kernel(dest_ref, h_ref, r_ref, p_ref, wg_ref, wp_ref, g_ref, b_ref, ys_hbm, y_ref, buf, sem,
                    *, bm, row0, n_blocks):
    i = pl.program_id(0)

    def row_copy(src, slot, k, r):
        return pltpu.make_async_copy(ys_hbm.at[pl.ds(src, 1)], buf.at[slot, k, pl.ds(r, 1)], sem.at[slot])

    def issue(blk, slot):
        def body(r4, c):
            for q in range(DMA_ISSUE_UNROLL // TOP_K):
                r = r4 * (DMA_ISSUE_UNROLL // TOP_K) + q
                a = TOP_K * (row0 + blk * bm + r)
                for k in range(TOP_K):
                    row_copy(dest_ref[a + k], slot, k, r).start(priority=k % DMA_PRIORITIES)
            return c
        lax.fori_loop(0, bm // (DMA_ISSUE_UNROLL // TOP_K), body, 0)

    @pl.when(i == 0)
    def _():
        issue(0, 0)

    @pl.when(i + 1 < n_blocks)
    def _():
        issue(i + 1, (i + 1) % 2)

    slot = i % 2
    for k in range(TOP_K):
        pltpu.make_async_copy(ys_hbm.at[pl.ds(0, bm)], buf.at[slot, k], sem.at[slot]).wait()

    h = h_ref[...]
    r = r_ref[...]
    f = buf[slot, 0] * r[:, 2:3] + buf[slot, 1] * r[:, 3:4]
    gate = jax.nn.sigmoid(_dot(h.astype(BF16), wg_ref[...]))
    proj = _dot(p_ref[...].astype(BF16), wp_ref[...])
    y_ref[...] = _ln(ALPHA * h + f + gate * proj, g_ref[...], b_ref[...])


def _combine_ple_ln2(dest, h, route, p, ys, wg, wp, g, b, *, row0, n_rows):
    bm = 256
    n_blocks = n_rows // bm
    b0 = row0 // bm
    vec = pl.BlockSpec((1, D_MODEL), lambda i, d: (0, 0))
    return pl.pallas_call(
        functools.partial(_combine_kernel, bm=bm, row0=row0, n_blocks=n_blocks),
        grid_spec=pltpu.PrefetchScalarGridSpec(
            num_scalar_prefetch=1,
            grid=(n_blocks,),
            in_specs=[pl.BlockSpec((bm, D_MODEL), lambda i, d: (b0 + i, 0)),
                      pl.BlockSpec((bm, LANES), lambda i, d: (b0 + i, 0)),
                      pl.BlockSpec((bm, PLE_DIM), lambda i, d: (i, 0)),
                      pl.BlockSpec((D_MODEL, D_MODEL), lambda i, d: (0, 0)),
                      pl.BlockSpec((PLE_DIM, D_MODEL), lambda i, d: (0, 0)),
                      vec, vec,
                      pl.BlockSpec(memory_space=pl.ANY)],
            out_specs=pl.BlockSpec((bm, D_MODEL), lambda i, d: (i, 0)),
            scratch_shapes=[pltpu.VMEM((2, TOP_K, bm, D_MODEL), F32), pltpu.SemaphoreType.DMA((2,))]),
        out_shape=jax.ShapeDtypeStruct((n_rows, D_MODEL), F32),
        compiler_params=_params(1, 48),
    )(dest, h, route, p, wg, wp, g, b, ys)


def _route_plan(route):
    n = route.shape[0]
    n_assign = n * TOP_K
    flat_e = route[:, :TOP_K].astype(jnp.int32).reshape(n_assign)
    onehot = (flat_e[:, None] == jnp.arange(N_EXPERTS, dtype=jnp.int32)[None, :]).astype(jnp.int32)
    csum = jnp.cumsum(onehot, axis=0)
    counts = csum[-1]
    rank = jnp.sum(csum * onehot, axis=1) - 1
    padded = (counts + MOE_ROWS - 1) // MOE_ROWS * MOE_ROWS
    pad_end = jnp.cumsum(padded)
    dest = ((pad_end - padded)[flat_e] + rank).astype(jnp.int32)
    n_blocks = -(-n_assign // MOE_ROWS) + N_EXPERTS - 1
    cap = n_blocks * MOE_ROWS
    slot_tok = jnp.zeros((cap,), jnp.int32).at[dest].set(jnp.arange(n_assign, dtype=jnp.int32) // TOP_K)
    block_e = jnp.minimum(jnp.searchsorted(pad_end, jnp.arange(n_blocks) * MOE_ROWS, side='right'),
                          N_EXPERTS - 1).astype(jnp.int32)
    n_used = (pad_end[-1:] // MOE_ROWS).astype(jnp.int32)
    return dest, slot_tok, block_e, n_used


def kernel(x_prompt, x_sample, state_conv, cache_k, cache_v, p_prompt, p_sample, conv_w_pw1, conv_b_pw1, conv_w_dw, conv_b_dw, conv_ln_g, conv_ln_b, conv_w_pw2, conv_b_pw2, w_kv, rel_bias, attn_w_q, attn_sinks, attn_w_o, ln1_g, ln1_b, ln2_g, ln2_b, ffn_w_gate, ffn_w_up, ffn_w_down, moe_w_router, moe_b_router, moe_w_gate, moe_w_up, moe_w_down, ple_w_gate, ple_w_proj):
    n_pb, p_seq, _ = x_prompt.shape
    n_sb, s_seq, _ = x_sample.shape
    n_p, n_s = n_pb * p_seq, n_sb * s_seq
    vec = lambda v: v.reshape(1, -1).astype(F32)

    xp = x_prompt.reshape(n_p, D_MODEL)
    xs = x_sample.reshape(n_s, D_MODEL)
    w1 = conv_w_pw1[0].astype(BF16)
    b1 = vec(conv_b_pw1[0])
    u_p = _pw1_glu(xp, w1, b1)
    u_s = _pw1_glu(xs, w1, b1)
    w_dw, b_dw = conv_w_dw[0].astype(F32), vec(conv_b_dw[0])
    state = jnp.pad(state_conv[0], ((0, 0), (CONV_HALO - (CONV_WIDTH - 1), 0), (0, 0))).reshape(n_sb * CONV_HALO, D_MODEL)
    yc_p = _dwconv(u_p, u_p, w_dw, b_dw, n_seq=n_pb, seq=p_seq, bt=256, zero_first=True)
    yc_s = _dwconv(u_s, state, w_dw, b_dw, n_seq=n_sb, seq=s_seq, bt=s_seq, zero_first=False)
    w2 = conv_w_pw2[0].astype(BF16)
    tail0 = (w2, vec(conv_b_pw2[0]), vec(conv_ln_g[0]), vec(conv_ln_b[0]), vec(ln1_g[0]), vec(ln1_b[0]))
    h, hb = _pw2_ln1(yc_p, yc_s, xp, xs, *tail0)
    f = _ffn(hb, ffn_w_gate[0].astype(BF16), ffn_w_up[0].astype(BF16), ffn_w_down[0].astype(BF16))
    x1, x1b = _ple_ln2(h, hb, f, p_prompt[0].reshape(n_p, PLE_DIM), p_sample[0].reshape(n_s, PLE_DIM),
                       ple_w_gate[0].astype(BF16), ple_w_proj[0].astype(BF16), vec(ln2_g[0]), vec(ln2_b[0]))

    q = _mm(x1b, attn_w_q[0].astype(BF16), scale=HEAD_DIM ** -0.5, out_dtype=BF16)
    kv = _mm(x1b, w_kv.astype(BF16), scale=1.0, out_dtype=F32)
    bias = _rel_bias(rel_bias)
    sinks = jnp.repeat(attn_sinks[0].astype(F32), CHUNK).reshape(N_KV_HEADS, GROUP * CHUNK, 1)
    o_p = _attention(q, kv, kv, kv, bias, sinks, n_seq=n_pb, seq=p_seq, tq=256, row0=0, mask_first=True)
    o_s = _attention(q, kv, cache_k.reshape(n_sb * WINDOW, KV_DIM), cache_v.reshape(n_sb * WINDOW, KV_DIM),
                     bias, sinks, n_seq=n_sb, seq=s_seq, tq=s_seq, row0=n_p, mask_first=False)
    h1, h1p = _oproj_ln1(o_p, o_s, x1, attn_w_o[0].astype(BF16), vec(ln1_g[1]), vec(ln1_b[1]))

    route = _router(h1, moe_w_router[0], moe_b_router[0])
    dest, slot_tok, block_e, n_used = _route_plan(route)
    xg = _gather_rows(slot_tok, h1p)
    a = _moe_up(block_e, n_used, xg, moe_w_gate[0], moe_w_up[0])
    ys = _moe_down(block_e, n_used, a, moe_w_down[0])
    tail1 = (ple_w_gate[1].astype(BF16), ple_w_proj[1].astype(BF16), vec(ln2_g[1]), vec(ln2_b[1]))
    y_p = _combine_ple_ln2(dest, h1, route, p_prompt[1].reshape(n_p, PLE_DIM), ys, *tail1, row0=0, n_rows=n_p)
    y_s = _combine_ple_ln2(dest, h1, route, p_sample[1].reshape(n_s, PLE_DIM), ys, *tail1, row0=n_p, n_rows=n_s)

    keep = CONV_WIDTH - 1
    conv_p = u_p.reshape(n_pb, p_seq, D_MODEL)[None, :, p_seq - keep:]
    conv_s = u_s.reshape(n_sb, s_seq, D_MODEL)[None, :, s_seq - keep:]
    kv_p = kv[:n_p].reshape(n_pb, p_seq, 2, N_KV_HEADS, HEAD_DIM)[:, p_seq - WINDOW:]
    kv_s = kv[n_p:].reshape(n_sb, s_seq, 2, N_KV_HEADS, HEAD_DIM)
    k_s = jnp.concatenate([cache_k, kv_s[:, :, 0]], axis=1)[:, s_seq:]
    v_s = jnp.concatenate([cache_v, kv_s[:, :, 1]], axis=1)[:, s_seq:]
    return (y_p.reshape(n_pb, p_seq, D_MODEL), y_s.reshape(n_sb, s_seq, D_MODEL), conv_p, conv_s,
            kv_p[:, :, 0], kv_p[:, :, 1], k_s, v_s)
```

```python
import functools
import math

import jax
import jax.numpy as jnp
from jax import lax
from jax.experimental import pallas as pl
from jax.experimental.pallas import tpu as pltpu

D_MODEL = 2048
CHUNK = 64
CONV_WIDTH = 31
HEAD_DIM = 64
N_HEADS = 32
N_KV_HEADS = 8
GROUP = N_HEADS // N_KV_HEADS
WINDOW = 128
BAND = WINDOW + CHUNK
NUM_BUCKETS = 32
MAX_DISTANCE = 128
D_FF = 5632
N_EXPERTS = 8
TOP_K = 2
PLE_DIM = 256
DEPTH = 2
ALPHA = (2 * DEPTH) ** 0.25
LN_EPS = 1e-5
NEG_INF = -1e30

KV_DIM = N_KV_HEADS * HEAD_DIM
CONV_HALO = 32
MOE_ROWS = 512
LANES = 128
SUBLANES = 8
PACKED_TILES = D_MODEL // 2 // LANES
DMA_ISSUE_UNROLL = 8
DMA_PRIORITIES = 2
MIB = 1024 * 1024

F32 = jnp.float32
BF16 = jnp.bfloat16


def _params(n_axes, vmem_mib):
    return pltpu.CompilerParams(dimension_semantics=("arbitrary",) * n_axes,
                                vmem_limit_bytes=vmem_mib * MIB)


def _dot(a, b):
    return jnp.dot(a, b, preferred_element_type=F32)


def _ln(x, g, b):
    mu = jnp.mean(x, axis=-1, keepdims=True)
    xc = x - mu
    var = jnp.mean(xc * xc, axis=-1, keepdims=True)
    return xc * lax.rsqrt(var + LN_EPS) * g + b


def _silu(x):
    return x * jax.nn.sigmoid(x)


def _pw1_glu_kernel(x_ref, wa_ref, wb_ref, ba_ref, bb_ref, u_ref, xb_ref):
    @pl.when(pl.program_id(1) == 0)
    def _():
        xb_ref[...] = x_ref[...].astype(BF16)

    xb = xb_ref[...]
    a = _dot(xb, wa_ref[...]) + ba_ref[...]
    g = _dot(xb, wb_ref[...]) + bb_ref[...]
    u_ref[...] = a * jax.nn.sigmoid(g)


def _pw1_glu(x, w, b):
    n = x.shape[0]
    bm, bn = math.gcd(n, 1024), 512
    nj = D_MODEL // bn
    return pl.pallas_call(
        _pw1_glu_kernel,
        grid=(n // bm, nj),
        in_specs=[pl.BlockSpec((bm, D_MODEL), lambda i, j: (i, 0)),
                  pl.BlockSpec((D_MODEL, bn), lambda i, j: (0, j)),
                  pl.BlockSpec((D_MODEL, bn), lambda i, j: (0, j + nj)),
                  pl.BlockSpec((1, bn), lambda i, j: (0, j)),
                  pl.BlockSpec((1, bn), lambda i, j: (0, j + nj))],
        out_specs=pl.BlockSpec((bm, bn), lambda i, j: (i, j)),
        out_shape=jax.ShapeDtypeStruct((n, D_MODEL), F32),
        scratch_shapes=[pltpu.VMEM((bm, D_MODEL), BF16)],
        compiler_params=_params(2, 48),
    )(x, w, w, b, b)


def _dwconv_kernel(prev_ref, cur_ref, w_ref, b_ref, y_ref, e_ref, *, bt, zero_first, rc):
    prev = prev_ref[...]
    if zero_first:
        prev = jnp.where(pl.program_id(1) == 0, 0.0, prev)
    e_ref[0, 0:CONV_HALO, :] = prev
    e_ref[0, CONV_HALO:, :] = cur_ref[...]
    e0 = e_ref[0]
    n_e = bt + CONV_HALO
    for rho in range(1, SUBLANES):
        e_ref[rho] = pltpu.roll(e0, n_e - rho, axis=0)
    off = CONV_HALO - (CONV_WIDTH - 1)
    for r0 in range(0, bt, rc):
        acc = jnp.broadcast_to(b_ref[...], (rc, b_ref.shape[1]))
        for k in range(CONV_WIDTH):
            rho = (k + off) % SUBLANES
            base = r0 + k + off - rho
            acc = acc + e_ref[rho, base:base + rc, :] * w_ref[k:k + 1, :]
        y_ref[r0:r0 + rc, :] = acc


def _dwconv(u, prev_src, w, b, *, n_seq, seq, bt, zero_first):
    bc = 512
    nt = seq // bt
    hb = bt // CONV_HALO
    if zero_first:
        prev_map = lambda s, t, c: (jnp.maximum((s * nt + t) * hb - 1, 0), c)
    else:
        prev_map = lambda s, t, c: (s, c)
    return pl.pallas_call(
        functools.partial(_dwconv_kernel, bt=bt, zero_first=zero_first, rc=32),
        grid=(n_seq, nt, D_MODEL // bc),
        in_specs=[pl.BlockSpec((CONV_HALO, bc), prev_map),
                  pl.BlockSpec((bt, bc), lambda s, t, c: (s * nt + t, c)),
                  pl.BlockSpec((CONV_WIDTH, bc), lambda s, t, c: (0, c)),
                  pl.BlockSpec((1, bc), lambda s, t, c: (0, c))],
        out_specs=pl.BlockSpec((bt, bc), lambda s, t, c: (s * nt + t, c)),
        out_shape=jax.ShapeDtypeStruct(u.shape, F32),
        scratch_shapes=[pltpu.VMEM((SUBLANES, bt + CONV_HALO, bc), F32)],
        compiler_params=_params(3, 32),
    )(prev_src, u, w, b)


def _two_group_specs(bm, width, n_first):
    nb = n_first // bm
    return (pl.BlockSpec((bm, width), lambda i, *_: (jnp.minimum(i, nb - 1), 0)),
            pl.BlockSpec((bm, width), lambda i, *_: (jnp.maximum(i - nb, 0), 0)))


def _pick_group(first_ref, second_ref, n_first_blocks):
    return jnp.where(pl.program_id(0) < n_first_blocks, first_ref[...], second_ref[...])


def _pw2_ln1_kernel(yp_ref, ys_ref, xp_ref, xs_ref, w_ref, b2_ref, cg_ref, cb_ref, g_ref, b_ref, h_ref, hb_ref,
                    *, nb_first):
    y = _silu(_ln(_pick_group(yp_ref, ys_ref, nb_first), cg_ref[...], cb_ref[...]))
    mix = _dot(y.astype(BF16), w_ref[...]) + b2_ref[...]
    h = _ln(ALPHA * _pick_group(xp_ref, xs_ref, nb_first) + mix, g_ref[...], b_ref[...])
    h_ref[...] = h
    hb_ref[...] = h.astype(BF16)


def _pw2_ln1(y_p, y_s, x_p, x_s, w, b2, cg, cb, g, b):
    n_p, n = y_p.shape[0], y_p.shape[0] + y_s.shape[0]
    bm = 256
    row = pl.BlockSpec((bm, D_MODEL), lambda i: (i, 0))
    vec = pl.BlockSpec((1, D_MODEL), lambda i: (0, 0))
    return pl.pallas_call(
        functools.partial(_pw2_ln1_kernel, nb_first=n_p // bm),
        grid=(n // bm,),
        in_specs=[*_two_group_specs(bm, D_MODEL, n_p), *_two_group_specs(bm, D_MODEL, n_p),
                  pl.BlockSpec((D_MODEL, D_MODEL), lambda i: (0, 0)), vec, vec, vec, vec, vec],
        out_specs=[row, row],
        out_shape=[jax.ShapeDtypeStruct((n, D_MODEL), F32), jax.ShapeDtypeStruct((n, D_MODEL), BF16)],
        compiler_params=_params(1, 56),
    )(y_p, y_s, x_p, x_s, w, b2, cg, cb, g, b)


def _ffn_kernel(x_ref, wg_ref, wu_ref, wd_ref, o_ref):
    j = pl.program_id(1)
    x = x_ref[...]
    g = _dot(x, wg_ref[...])
    u = _dot(x, wu_ref[...])
    part = _dot((_silu(g) * u).astype(BF16), wd_ref[...])

    @pl.when(j == 0)
    def _():
        o_ref[...] = part

    @pl.when(j > 0)
    def _():
        o_ref[...] += part


def _ffn(xb, wg, wu, wd):
    n = xb.shape[0]
    bm, bf = math.gcd(n, 1024), 512
    return pl.pallas_call(
        _ffn_kernel,
        grid=(n // bm, D_FF // bf),
        in_specs=[pl.BlockSpec((bm, D_MODEL), lambda i, j: (i, 0)),
                  pl.BlockSpec((D_MODEL, bf), lambda i, j: (0, j)),
                  pl.BlockSpec((D_MODEL, bf), lambda i, j: (0, j)),
                  pl.BlockSpec((bf, D_MODEL), lambda i, j: (j, 0))],
        out_specs=pl.BlockSpec((bm, D_MODEL), lambda i, j: (i, 0)),
        out_shape=jax.ShapeDtypeStruct((n, D_MODEL), F32),
        compiler_params=_params(2, 60),
    )(xb, wg, wu, wd)


def _ple_ln2_kernel(h_ref, hb_ref, f_ref, pp_ref, ps_ref, wg_ref, wp_ref, g_ref, b_ref, x_ref, xb_ref, *, nb_first):
    gate = jax.nn.sigmoid(_dot(hb_ref[...], wg_ref[...]))
    proj = _dot(_pick_group(pp_ref, ps_ref, nb_first).astype(BF16), wp_ref[...])
    x = _ln(ALPHA * h_ref[...] + f_ref[...] + gate * proj, g_ref[...], b_ref[...])
    x_ref[...] = x
    xb_ref[...] = x.astype(BF16)


def _ple_ln2(h, hb, f, p_p, p_s, wg, wp, g, b):
    n = h.shape[0]
    bm = 256
    row = pl.BlockSpec((bm, D_MODEL), lambda i: (i, 0))
    vec = pl.BlockSpec((1, D_MODEL), lambda i: (0, 0))
    return pl.pallas_call(
        functools.partial(_ple_ln2_kernel, nb_first=p_p.shape[0] // bm),
        grid=(n // bm,),
        in_specs=[row, row, row, *_two_group_specs(bm, PLE_DIM, p_p.shape[0]),
                  pl.BlockSpec((D_MODEL, D_MODEL), lambda i: (0, 0)),
                  pl.BlockSpec((PLE_DIM, D_MODEL), lambda i: (0, 0)), vec, vec],
        out_specs=[row, row],
        out_shape=[jax.ShapeDtypeStruct((n, D_MODEL), F32), jax.ShapeDtypeStruct((n, D_MODEL), BF16)],
        compiler_params=_params(1, 48),
    )(h, hb, f, p_p, p_s, wg, wp, g, b)


def _mm_kernel(x_ref, w_ref, o_ref, *, scale):
    acc = _dot(x_ref[...], w_ref[...])
    if scale != 1.0:
        acc = acc * scale
    o_ref[...] = acc.astype(o_ref.dtype)


def _mm(xb, w, *, scale, out_dtype):
    n, k = xb.shape
    m = w.shape[1]
    bm, bn = math.gcd(n, 1024), 512
    return pl.pallas_call(
        functools.partial(_mm_kernel, scale=scale),
        grid=(n // bm, m // bn),
        in_specs=[pl.BlockSpec((bm, k), lambda i, j: (i, 0)),
                  pl.BlockSpec((k, bn), lambda i, j: (0, j))],
        out_specs=pl.BlockSpec((bm, bn), lambda i, j: (i, j)),
        out_shape=jax.ShapeDtypeStruct((n, m), out_dtype),
        compiler_params=_params(2, 32),
    )(xb, w)


def _t5_bucket(rel):
    half = NUM_BUCKETS // 2
    exact = half // 2
    ret = jnp.where(rel > 0, half, 0)
    n = jnp.abs(rel)
    large = exact + (jnp.log(jnp.maximum(n, 1).astype(F32) / exact)
                     / math.log(MAX_DISTANCE / exact) * (half - exact)).astype(jnp.int32)
    large = jnp.minimum(large, half - 1)
    return ret + jnp.where(n < exact, n, large)


def _bias_kernel(table_ref, bucket_ref, o_ref):
    bk = bucket_ref[...]
    masks = [bk == b for b in range(NUM_BUCKETS)]
    for h in range(N_HEADS):
        acc = jnp.zeros(bk.shape, F32)
        for b in range(NUM_BUCKETS):
            acc = jnp.where(masks[b], table_ref[b, h], acc)
        o_ref[h] = acc


def _rel_bias(table):
    i = jnp.arange(CHUNK)[:, None]
    j = jnp.arange(BAND)[None, :]
    bucket = _t5_bucket((j - WINDOW) - i).astype(jnp.int32)
    out = pl.pallas_call(
        _bias_kernel,
        in_specs=[pl.BlockSpec(memory_space=pltpu.SMEM), pl.BlockSpec(memory_space=pltpu.VMEM)],
        out_specs=pl.BlockSpec(memory_space=pltpu.VMEM),
        out_shape=jax.ShapeDtypeStruct((N_HEADS, CHUNK, BAND), F32),
    )(table.astype(F32), bucket)
    return out.reshape(N_KV_HEADS, GROUP * CHUNK, BAND)


def _attn_kernel(q_ref, kp_ref, vp_ref, kc_ref, vc_ref, bias_ref, sink_ref, o_ref, q_scr, k_scr, v_scr,
                 *, tq, mask_first):
    t = pl.program_id(1)
    for h in range(N_KV_HEADS):
        hs = slice(h * HEAD_DIM, (h + 1) * HEAD_DIM)
        k_scr[h, 0:WINDOW, :] = kp_ref[:, hs].astype(BF16)
        k_scr[h, WINDOW:, :] = kc_ref[:, hs].astype(BF16)
        v_scr[h, 0:WINDOW, :] = vp_ref[:, hs].astype(BF16)
        v_scr[h, WINDOW:, :] = vc_ref[:, hs].astype(BF16)
    for h in range(N_HEADS):
        q_scr[h] = q_ref[:, h * HEAD_DIM:(h + 1) * HEAD_DIM]
    rows = GROUP * CHUNK
    bias = bias_ref[...]
    sink = sink_ref[...]
    for c in range(tq // CHUNK):
        cs = slice(c * CHUNK, (c + 1) * CHUNK)
        ks = slice(c * CHUNK, c * CHUNK + BAND)
        qh = q_scr[:, cs, :].reshape(N_KV_HEADS, rows, HEAD_DIM)
        s = jnp.einsum('hqd,hkd->hqk', qh, k_scr[:, ks, :], preferred_element_type=F32) + bias
        if mask_first:
            first_valid = WINDOW - (t * tq + c * CHUNK)
            kidx = lax.broadcasted_iota(jnp.int32, s.shape, 2)
            s = jnp.where(kidx >= first_valid, s, NEG_INF)
        m = jnp.maximum(jnp.max(s, axis=-1, keepdims=True), sink)
        e = jnp.exp(s - m)
        den = jnp.sum(e, axis=-1, keepdims=True) + jnp.exp(sink - m)
        p = (e * (1.0 / den)).astype(BF16)
        o = jnp.einsum('hqk,hkd->hqd', p, v_scr[:, ks, :], preferred_element_type=F32)
        o = o.reshape(N_HEADS, CHUNK, HEAD_DIM).astype(o_ref.dtype)
        for h in range(N_HEADS):
            o_ref[cs, h * HEAD_DIM:(h + 1) * HEAD_DIM] = o[h]


def _attention(q, kv, k_prev, v_prev, bias, sinks, *, n_seq, seq, tq, row0, mask_first):
    nt = seq // tq
    qb0 = row0 // tq
    if mask_first:
        wpt = tq // WINDOW
        kp_spec = pl.BlockSpec((WINDOW, KV_DIM), lambda s, t: (jnp.maximum((s * nt + t) * wpt - 1, 0), 0))
        vp_spec = pl.BlockSpec((WINDOW, KV_DIM), lambda s, t: (jnp.maximum((s * nt + t) * wpt - 1, 0), 1))
    else:
        kp_spec = pl.BlockSpec((WINDOW, KV_DIM), lambda s, t: (s, 0))
        vp_spec = pl.BlockSpec((WINDOW, KV_DIM), lambda s, t: (s, 0))
    return pl.pallas_call(
        functools.partial(_attn_kernel, tq=tq, mask_first=mask_first),
        grid=(n_seq, nt),
        in_specs=[pl.BlockSpec((tq, D_MODEL), lambda s, t: (qb0 + s * nt + t, 0)),
                  kp_spec, vp_spec,
                  pl.BlockSpec((tq, KV_DIM), lambda s, t: (qb0 + s * nt + t, 0)),
                  pl.BlockSpec((tq, KV_DIM), lambda s, t: (qb0 + s * nt + t, 1)),
                  pl.BlockSpec((N_KV_HEADS, GROUP * CHUNK, BAND), lambda s, t: (0, 0, 0)),
                  pl.BlockSpec((N_KV_HEADS, GROUP * CHUNK, 1), lambda s, t: (0, 0, 0))],
        out_specs=pl.BlockSpec((tq, D_MODEL), lambda s, t: (s * nt + t, 0)),
        out_shape=jax.ShapeDtypeStruct((n_seq * seq, D_MODEL), BF16),
        scratch_shapes=[pltpu.VMEM((N_HEADS, tq, HEAD_DIM), BF16),
                        pltpu.VMEM((N_KV_HEADS, WINDOW + tq, HEAD_DIM), BF16),
                        pltpu.VMEM((N_KV_HEADS, WINDOW + tq, HEAD_DIM), BF16)],
        compiler_params=_params(2, 40),
    )(q, k_prev, v_prev, kv, kv, bias, sinks)


def _f32_bits(x):
    return lax.bitcast_convert_type(x, jnp.uint32)


def _pack_bf16_pairs(x):
    c = x.shape[1] // 2
    lo = _f32_bits(x[:, :c].astype(BF16).astype(F32))
    hi = _f32_bits(x[:, c:].astype(BF16).astype(F32))
    return hi | (lo >> 16)


def _unpack_bf16_pairs(w):
    lo = lax.bitcast_convert_type(w << 16, F32).astype(BF16)
    hi = lax.bitcast_convert_type(w & jnp.uint32(0xFFFF0000), F32).astype(BF16)
    return lo, hi


def _route(h, wh, wl, b):
    hh = h.astype(BF16)
    hl = (h - hh.astype(F32)).astype(BF16)
    logits = _dot(hh, wh) + (_dot(hl, wh) + _dot(hh, wl)) + b
    lane = lax.broadcasted_iota(jnp.int32, logits.shape, 1).astype(F32)
    ninf = -jnp.inf
    lg = jnp.where(lane < N_EXPERTS, logits, ninf)
    m1 = jnp.max(lg, axis=-1, keepdims=True)
    i1 = jnp.min(jnp.where(lg == m1, lane, float(LANES)), axis=-1, keepdims=True)
    lg2 = jnp.where(lane == i1, ninf, lg)
    m2 = jnp.max(lg2, axis=-1, keepdims=True)
    i2 = jnp.min(jnp.where(lg2 == m2, lane, float(LANES)), axis=-1, keepdims=True)
    ex = jnp.exp(m2 - m1)
    den = 1.0 + ex
    out = jnp.where(lane == 0, i1, jnp.where(lane == 1, i2, jnp.where(lane == 2, 1.0 / den, ex / den)))
    return jnp.where(lane < 4, out, 0.0)


def _oproj_ln1_kernel(op_ref, os_ref, x_ref, w_ref, g_ref, b_ref, rwh_ref, rwl_ref, rb_ref, h_ref, hp_ref, r_ref,
                      *, nb_first):
    mix = _dot(_pick_group(op_ref, os_ref, nb_first), w_ref[...])
    h = _ln(ALPHA * x_ref[...] + mix, g_ref[...], b_ref[...])
    h_ref[...] = h
    r_ref[...] = _route(h, rwh_ref[...], rwl_ref[...], rb_ref[...])
    packed = _pack_bf16_pairs(h)
    for j in range(PACKED_TILES):
        hp_ref[pl.ds(j, h.shape[0], stride=PACKED_TILES), :] = packed[:, j * LANES:(j + 1) * LANES]


def _oproj_ln1_route(o_p, o_s, x, w, g, b, w_router, b_router):
    n = x.shape[0]
    bm = 256
    row = pl.BlockSpec((bm, D_MODEL), lambda i: (i, 0))
    vec = pl.BlockSpec((1, D_MODEL), lambda i: (0, 0))
    rw = pl.BlockSpec((D_MODEL, LANES), lambda i: (0, 0))
    wp = jnp.zeros((D_MODEL, LANES), F32).at[:, :N_EXPERTS].set(w_router.astype(F32))
    wh = wp.astype(BF16)
    wl = (wp - wh.astype(F32)).astype(BF16)
    bp = jnp.zeros((1, LANES), F32).at[0, :N_EXPERTS].set(b_router.astype(F32))
    return pl.pallas_call(
        functools.partial(_oproj_ln1_kernel, nb_first=o_p.shape[0] // bm),
        grid=(n // bm,),
        in_specs=[*_two_group_specs(bm, D_MODEL, o_p.shape[0]), row,
                  pl.BlockSpec((D_MODEL, D_MODEL), lambda i: (0, 0)), vec, vec,
                  rw, rw, pl.BlockSpec((1, LANES), lambda i: (0, 0))],
        out_specs=[row, pl.BlockSpec((bm * PACKED_TILES, LANES), lambda i: (i, 0)),
                   pl.BlockSpec((bm, LANES), lambda i: (i, 0))],
        out_shape=[jax.ShapeDtypeStruct((n, D_MODEL), F32),
                   jax.ShapeDtypeStruct((n * PACKED_TILES, LANES), jnp.uint32),
                   jax.ShapeDtypeStruct((n, LANES), F32)],
        compiler_params=_params(1, 48),
    )(o_p, o_s, x, w, g, b, wh, wl, bp)


def _gather_kernel(tok_ref, h_hbm, o_ref, buf, sem, *, rows, n_blocks):
    i = pl.program_id(0)

    def row_copy(tok, slot, r):
        src = pl.multiple_of(tok * PACKED_TILES, PACKED_TILES)
        dst = pl.multiple_of(r * PACKED_TILES, PACKED_TILES)
        return pltpu.make_async_copy(h_hbm.at[pl.ds(src, PACKED_TILES)], buf.at[slot, pl.ds(dst, PACKED_TILES)],
                                     sem.at[slot])

    def issue(blk, slot):
        def body(r8, c):
            for q in range(DMA_ISSUE_UNROLL):
                r = r8 * DMA_ISSUE_UNROLL + q
                row_copy(tok_ref[blk * rows + r], slot, r).start(priority=q % DMA_PRIORITIES)
            return c
        lax.fori_loop(0, rows // DMA_ISSUE_UNROLL, body, 0)

    @pl.when(i == 0)
    def _():
        issue(0, 0)

    @pl.when(i + 1 < n_blocks)
    def _():
        issue(i + 1, (i + 1) % 2)

    slot = i % 2
    pltpu.make_async_copy(h_hbm.at[pl.ds(0, rows * PACKED_TILES)], buf.at[slot], sem.at[slot]).wait()
    half = D_MODEL // 2
    for j in range(PACKED_TILES):
        lo, hi = _unpack_bf16_pairs(buf[slot, pl.ds(j, rows, stride=PACKED_TILES), :])
        o_ref[:, j * LANES:(j + 1) * LANES] = lo
        o_ref[:, half + j * LANES:half + (j + 1) * LANES] = hi


def _gather_rows(slot_tok, hp):
    cap = slot_tok.shape[0]
    rows = MOE_ROWS
    n_blocks = cap // rows
    return pl.pallas_call(
        functools.partial(_gather_kernel, rows=rows, n_blocks=n_blocks),
        grid_spec=pltpu.PrefetchScalarGridSpec(
            num_scalar_prefetch=1,
            grid=(n_blocks,),
            in_specs=[pl.BlockSpec(memory_space=pl.ANY)],
            out_specs=pl.BlockSpec((rows, D_MODEL), lambda i, tok: (i, 0)),
            scratch_shapes=[pltpu.VMEM((2, rows * PACKED_TILES, LANES), jnp.uint32),
                            pltpu.SemaphoreType.DMA((2,))]),
        out_shape=jax.ShapeDtypeStruct((cap, D_MODEL), BF16),
        compiler_params=_params(1, 32),
    )(slot_tok, hp)


def _expert_changed(be_ref, i):
    return jnp.logical_or(i == 0, be_ref[i] != be_ref[jnp.maximum(i - 1, 0)])


def _moe_up_kernel(be_ref, nu_ref, x_ref, wg_ref, wu_ref, a_ref, wgb_ref, wub_ref):
    i = pl.program_id(1)
    used = i < nu_ref[0]

    @pl.when(jnp.logical_and(used, _expert_changed(be_ref, i)))
    def _():
        wgb_ref[...] = wg_ref[...].astype(BF16)
        wub_ref[...] = wu_ref[...].astype(BF16)

    @pl.when(used)
    def _():
        x = x_ref[...]
        g = _dot(x, wgb_ref[...])
        u = _dot(x, wub_ref[...])
        a_ref[...] = (_silu(g) * u).astype(a_ref.dtype)

    @pl.when(jnp.logical_not(used))
    def _():
        a_ref[...] = jnp.zeros(a_ref.shape, a_ref.dtype)


def _moe_up(block_e, n_used, xs, wg, wu):
    cap = xs.shape[0]
    bf = 512
    return pl.pallas_call(
        _moe_up_kernel,
        grid_spec=pltpu.PrefetchScalarGridSpec(
            num_scalar_prefetch=2,
            grid=(D_FF // bf, cap // MOE_ROWS),
            in_specs=[pl.BlockSpec((MOE_ROWS, D_MODEL), lambda j, i, be, nu: (i, 0)),
                      pl.BlockSpec((None, D_MODEL, bf), lambda j, i, be, nu: (be[i], 0, j)),
                      pl.BlockSpec((None, D_MODEL, bf), lambda j, i, be, nu: (be[i], 0, j))],
            out_specs=pl.BlockSpec((MOE_ROWS, bf), lambda j, i, be, nu: (i, j)),
            scratch_shapes=[pltpu.VMEM((D_MODEL, bf), BF16), pltpu.VMEM((D_MODEL, bf), BF16)]),
        out_shape=jax.ShapeDtypeStruct((cap, D_FF), BF16),
        compiler_params=_params(2, 48),
    )(block_e, n_used, xs, wg, wu)


def _moe_down_kernel(be_ref, nu_ref, a_ref, wd_ref, y_ref, wdb_ref):
    i = pl.program_id(1)
    used = i < nu_ref[0]

    @pl.when(jnp.logical_and(used, _expert_changed(be_ref, i)))
    def _():
        wdb_ref[...] = wd_ref[...].astype(BF16)

    @pl.when(used)
    def _():
        y_ref[...] = _dot(a_ref[...], wdb_ref[...])

    @pl.when(jnp.logical_not(used))
    def _():
        y_ref[...] = jnp.zeros(y_ref.shape, y_ref.dtype)


def _moe_down(block_e, n_used, a, wd):
    cap = a.shape[0]
    bn = 512
    return pl.pallas_call(
        _moe_down_kernel,
        grid_spec=pltpu.PrefetchScalarGridSpec(
            num_scalar_prefetch=2,
            grid=(D_MODEL // bn, cap // MOE_ROWS),
            in_specs=[pl.BlockSpec((MOE_ROWS, D_FF), lambda j, i, be, nu: (i, 0)),
                      pl.BlockSpec((None, D_FF, bn), lambda j, i, be, nu: (be[i], 0, j))],
            out_specs=pl.BlockSpec((MOE_ROWS, bn), lambda j, i, be, nu: (i, j)),
            scratch_shapes=[pltpu.VMEM((D_FF, bn), BF16)]),
        out_shape=jax.ShapeDtypeStruct((cap, D_MODEL), F32),
        compiler_params=_params(2, 56),
    )(block_e, n_used, a, wd)


def _combine_kernel(dest_ref, h_ref, r_ref, p_ref, wg_ref, wp_ref, g_ref, b_ref, ys_hbm, y_ref, buf, sem,
                    *, bm, row0, n_blocks):
    i = pl.program_id(0)

    def row_copy(src, slot, k, r):
        return pltpu.make_async_copy(ys_hbm.at[pl.ds(src, 1)], buf.at[slot, k, pl.ds(r, 1)], sem.at[slot])

    def issue(blk, slot):
        def body(r4, c):
            for q in range(DMA_ISSUE_UNROLL // TOP_K):
                r = r4 * (DMA_ISSUE_UNROLL // TOP_K) + q
                a = TOP_K * (row0 + blk * bm + r)
                for k in range(TOP_K):
                    row_copy(dest_ref[a + k], slot, k, r).start(priority=k % DMA_PRIORITIES)
            return c
        lax.fori_loop(0, bm // (DMA_ISSUE_UNROLL // TOP_K), body, 0)

    @pl.when(i == 0)
    def _():
        issue(0, 0)

    @pl.when(i + 1 < n_blocks)
    def _():
        issue(i + 1, (i + 1) % 2)

    slot = i % 2
    for k in range(TOP_K):
        pltpu.make_async_copy(ys_hbm.at[pl.ds(0, bm)], buf.at[slot, k], sem.at[slot]).wait()

    h = h_ref[...]
    r = r_ref[...]
    f = buf[slot, 0] * r[:, 2:3] + buf[slot, 1] * r[:, 3:4]
    gate = jax.nn.sigmoid(_dot(h.astype(BF16), wg_ref[...]))
    proj = _dot(p_ref[...].astype(BF16), wp_ref[...])
    y_ref[...] = _ln(ALPHA * h + f + gate * proj, g_ref[...], b_ref[...])


def _combine_ple_ln2(dest, h, route, p, ys, wg, wp, g, b, *, row0, n_rows):
    bm = 256
    n_blocks = n_rows // bm
    b0 = row0 // bm
    vec = pl.BlockSpec((1, D_MODEL), lambda i, d: (0, 0))
    return pl.pallas_call(
        functools.partial(_combine_kernel, bm=bm, row0=row0, n_blocks=n_blocks),
        grid_spec=pltpu.PrefetchScalarGridSpec(
            num_scalar_prefetch=1,
            grid=(n_blocks,),
            in_specs=[pl.BlockSpec((bm, D_MODEL), lambda i, d: (b0 + i, 0)),
                      pl.BlockSpec((bm, LANES), lambda i, d: (b0 + i, 0)),
                      pl.BlockSpec((bm, PLE_DIM), lambda i, d: (i, 0)),
                      pl.BlockSpec((D_MODEL, D_MODEL), lambda i, d: (0, 0)),
                      pl.BlockSpec((PLE_DIM, D_MODEL), lambda i, d: (0, 0)),
                      vec, vec,
                      pl.BlockSpec(memory_space=pl.ANY)],
            out_specs=pl.BlockSpec((bm, D_MODEL), lambda i, d: (i, 0)),
            scratch_shapes=[pltpu.VMEM((2, TOP_K, bm, D_MODEL), F32), pltpu.SemaphoreType.DMA((2,))]),
        out_shape=jax.ShapeDtypeStruct((n_rows, D_MODEL), F32),
        compiler_params=_params(1, 48),
    )(dest, h, route, p, wg, wp, g, b, ys)


def _route_plan(route):
    n = route.shape[0]
    n_assign = n * TOP_K
    flat_e = route[:, :TOP_K].astype(jnp.int32).reshape(n_assign)
    onehot = (flat_e[:, None] == jnp.arange(N_EXPERTS, dtype=jnp.int32)[None, :]).astype(jnp.int32)
    csum = jnp.cumsum(onehot, axis=0)
    counts = csum[-1]
    rank = jnp.sum(csum * onehot, axis=1) - 1
    padded = (counts + MOE_ROWS - 1) // MOE_ROWS * MOE_ROWS
    pad_end = jnp.cumsum(padded)
    pad_start = jnp.sum(onehot * (pad_end - padded)[None, :], axis=1)
    dest = (pad_start + rank).astype(jnp.int32)
    n_blocks = -(-n_assign // MOE_ROWS) + N_EXPERTS - 1
    cap = n_blocks * MOE_ROWS
    slot_tok = jnp.zeros((cap,), jnp.int32).at[dest].set(jnp.arange(n_assign, dtype=jnp.int32) // TOP_K)
    block_start = jnp.arange(n_blocks, dtype=pad_end.dtype) * MOE_ROWS
    block_e = jnp.minimum(jnp.sum(pad_end[None, :] <= block_start[:, None], axis=1), N_EXPERTS - 1).astype(jnp.int32)
    n_used = (pad_end[-1:] // MOE_ROWS).astype(jnp.int32)
    return dest, slot_tok, block_e, n_used


def kernel(x_prompt, x_sample, state_conv, cache_k, cache_v, p_prompt, p_sample, conv_w_pw1, conv_b_pw1, conv_w_dw, conv_b_dw, conv_ln_g, conv_ln_b, conv_w_pw2, conv_b_pw2, w_kv, rel_bias, attn_w_q, attn_sinks, attn_w_o, ln1_g, ln1_b, ln2_g, ln2_b, ffn_w_gate, ffn_w_up, ffn_w_down, moe_w_router, moe_b_router, moe_w_gate, moe_w_up, moe_w_down, ple_w_gate, ple_w_proj):
    n_pb, p_seq, _ = x_prompt.shape
    n_sb, s_seq, _ = x_sample.shape
    n_p, n_s = n_pb * p_seq, n_sb * s_seq
    vec = lambda v: v.reshape(1, -1).astype(F32)

    xp = x_prompt.reshape(n_p, D_MODEL)
    xs = x_sample.reshape(n_s, D_MODEL)
    w1 = conv_w_pw1[0].astype(BF16)
    b1 = vec(conv_b_pw1[0])
    u_p = _pw1_glu(xp, w1, b1)
    u_s = _pw1_glu(xs, w1, b1)
    w_dw, b_dw = conv_w_dw[0].astype(F32), vec(conv_b_dw[0])
    state = jnp.pad(state_conv[0], ((0, 0), (CONV_HALO - (CONV_WIDTH - 1), 0), (0, 0))).reshape(n_sb * CONV_HALO, D_MODEL)
    yc_p = _dwconv(u_p, u_p, w_dw, b_dw, n_seq=n_pb, seq=p_seq, bt=256, zero_first=True)
    yc_s = _dwconv(u_s, state, w_dw, b_dw, n_seq=n_sb, seq=s_seq, bt=s_seq, zero_first=False)
    w2 = conv_w_pw2[0].astype(BF16)
    tail0 = (w2, vec(conv_b_pw2[0]), vec(conv_ln_g[0]), vec(conv_ln_b[0]), vec(ln1_g[0]), vec(ln1_b[0]))
    h, hb = _pw2_ln1(yc_p, yc_s, xp, xs, *tail0)
    f = _ffn(hb, ffn_w_gate[0].astype(BF16), ffn_w_up[0].astype(BF16), ffn_w_down[0].astype(BF16))
    x1, x1b = _ple_ln2(h, hb, f, p_prompt[0].reshape(n_p, PLE_DIM), p_sample[0].reshape(n_s, PLE_DIM),
                       ple_w_gate[0].astype(BF16), ple_w_proj[0].astype(BF16), vec(ln2_g[0]), vec(ln2_b[0]))

    q = _mm(x1b, attn_w_q[0].astype(BF16), scale=HEAD_DIM ** -0.5, out_dtype=BF16)
    kv = _mm(x1b, w_kv.astype(BF16), scale=1.0, out_dtype=F32)
    bias = _rel_bias(rel_bias)
    sinks = jnp.repeat(attn_sinks[0].astype(F32), CHUNK).reshape(N_KV_HEADS, GROUP * CHUNK, 1)
    o_p = _attention(q, kv, kv, kv, bias, sinks, n_seq=n_pb, seq=p_seq, tq=256, row0=0, mask_first=True)
    o_s = _attention(q, kv, cache_k.reshape(n_sb * WINDOW, KV_DIM), cache_v.reshape(n_sb * WINDOW, KV_DIM),
                     bias, sinks, n_seq=n_sb, seq=s_seq, tq=s_seq, row0=n_p, mask_first=False)
    h1, h1p, route = _oproj_ln1_route(o_p, o_s, x1, attn_w_o[0].astype(BF16), vec(ln1_g[1]), vec(ln1_b[1]),
                                      moe_w_router[0], moe_b_router[0])

    dest, slot_tok, block_e, n_used = _route_plan(route)
    xg = _gather_rows(slot_tok, h1p)
    a = _moe_up(block_e, n_used, xg, moe_w_gate[0], moe_w_up[0])
    ys = _moe_down(block_e, n_used, a, moe_w_down[0])
    tail1 = (ple_w_gate[1].astype(BF16), ple_w_proj[1].astype(BF16), vec(ln2_g[1]), vec(ln2_b[1]))
    y_p = _combine_ple_ln2(dest, h1, route, p_prompt[1].reshape(n_p, PLE_DIM), ys, *tail1, row0=0, n_rows=n_p)
    y_s = _combine_ple_ln2(dest, h1, route, p_sample[1].reshape(n_s, PLE_DIM), ys, *tail1, row0=n_p, n_rows=n_s)

    keep = CONV_WIDTH - 1
    conv_p = u_p.reshape(n_pb, p_seq, D_MODEL)[None, :, p_seq - keep:]
    conv_s = u_s.reshape(n_sb, s_seq, D_MODEL)[None, :, s_seq - keep:]
    wps = p_seq // WINDOW
    kv_p = kv.reshape(-1, WINDOW, 2 * KV_DIM)[wps - 1:n_pb * wps:wps].reshape(n_pb, WINDOW, 2, N_KV_HEADS, HEAD_DIM)
    kv_s = kv[n_p:].reshape(n_sb, s_seq, 2, N_KV_HEADS, HEAD_DIM)
    k_s = jnp.concatenate([cache_k, kv_s[:, :, 0]], axis=1)[:, s_seq:]
    v_s = jnp.concatenate([cache_v, kv_s[:, :, 1]], axis=1)[:, s_seq:]
    return (y_p.reshape(n_pb, p_seq, D_MODEL), y_s.reshape(n_sb, s_seq, D_MODEL), conv_p, conv_s,
            kv_p[:, :, 0], kv_p[:, :, 1], k_s, v_s)
```

```python
import functools
import math

import jax
import jax.numpy as jnp
from jax import lax
from jax.experimental import pallas as pl
from jax.experimental.pallas import tpu as pltpu

D_MODEL = 2048
CHUNK = 64
CONV_WIDTH = 31
HEAD_DIM = 64
N_HEADS = 32
N_KV_HEADS = 8
GROUP = N_HEADS // N_KV_HEADS
WINDOW = 128
BAND = WINDOW + CHUNK
NUM_BUCKETS = 32
MAX_DISTANCE = 128
D_FF = 5632
N_EXPERTS = 8
TOP_K = 2
PLE_DIM = 256
DEPTH = 2
ALPHA = (2 * DEPTH) ** 0.25
LN_EPS = 1e-5
NEG_INF = -1e30

KV_DIM = N_KV_HEADS * HEAD_DIM
CONV_HALO = 32
MOE_ROWS = 1024
MOE_SUB_ROWS = 512
LANES = 128
SUBLANES = 8
PACKED_TILES = D_MODEL // 2 // LANES
DMA_ISSUE_UNROLL = 8
DMA_PRIORITIES = 2
MIB = 1024 * 1024

F32 = jnp.float32
BF16 = jnp.bfloat16


def _params(n_axes, vmem_mib):
    return pltpu.CompilerParams(dimension_semantics=("arbitrary",) * n_axes,
                                vmem_limit_bytes=vmem_mib * MIB)


def _dot(a, b):
    return jnp.dot(a, b, preferred_element_type=F32)


def _ln(x, g, b):
    mu = jnp.mean(x, axis=-1, keepdims=True)
    xc = x - mu
    var = jnp.mean(xc * xc, axis=-1, keepdims=True)
    return xc * lax.rsqrt(var + LN_EPS) * g + b


def _silu(x):
    return x * jax.nn.sigmoid(x)


def _pw1_glu_kernel(x_ref, wa_ref, wb_ref, ba_ref, bb_ref, u_ref, xb_ref):
    @pl.when(pl.program_id(1) == 0)
    def _():
        xb_ref[...] = x_ref[...].astype(BF16)

    xb = xb_ref[...]
    a = _dot(xb, wa_ref[...]) + ba_ref[...]
    g = _dot(xb, wb_ref[...]) + bb_ref[...]
    u_ref[...] = a * jax.nn.sigmoid(g)


def _pw1_glu(x, w, b):
    n = x.shape[0]
    bm, bn = math.gcd(n, 1024), 512
    nj = D_MODEL // bn
    return pl.pallas_call(
        _pw1_glu_kernel,
        grid=(n // bm, nj),
        in_specs=[pl.BlockSpec((bm, D_MODEL), lambda i, j: (i, 0)),
                  pl.BlockSpec((D_MODEL, bn), lambda i, j: (0, j)),
                  pl.BlockSpec((D_MODEL, bn), lambda i, j: (0, j + nj)),
                  pl.BlockSpec((1, bn), lambda i, j: (0, j)),
                  pl.BlockSpec((1, bn), lambda i, j: (0, j + nj))],
        out_specs=pl.BlockSpec((bm, bn), lambda i, j: (i, j)),
        out_shape=jax.ShapeDtypeStruct((n, D_MODEL), F32),
        scratch_shapes=[pltpu.VMEM((bm, D_MODEL), BF16)],
        compiler_params=_params(2, 48),
    )(x, w, w, b, b)


def _dwconv_kernel(prev_ref, cur_ref, w_ref, b_ref, y_ref, e_ref, *, bt, zero_first, rc):
    prev = prev_ref[...]
    if zero_first:
        prev = jnp.where(pl.program_id(1) == 0, 0.0, prev)
    e_ref[0, 0:CONV_HALO, :] = prev
    e_ref[0, CONV_HALO:, :] = cur_ref[...]
    e0 = e_ref[0]
    n_e = bt + CONV_HALO
    for rho in range(1, SUBLANES):
        e_ref[rho] = pltpu.roll(e0, n_e - rho, axis=0)
    off = CONV_HALO - (CONV_WIDTH - 1)
    for r0 in range(0, bt, rc):
        acc = jnp.broadcast_to(b_ref[...], (rc, b_ref.shape[1]))
        for k in range(CONV_WIDTH):
            rho = (k + off) % SUBLANES
            base = r0 + k + off - rho
            acc = acc + e_ref[rho, base:base + rc, :] * w_ref[k:k + 1, :]
        y_ref[r0:r0 + rc, :] = acc


def _dwconv(u, prev_src, w, b, *, n_seq, seq, bt, zero_first):
    bc = 512
    nt = seq // bt
    hb = bt // CONV_HALO
    if zero_first:
        prev_map = lambda s, t, c: (jnp.maximum((s * nt + t) * hb - 1, 0), c)
    else:
        prev_map = lambda s, t, c: (s, c)
    return pl.pallas_call(
        functools.partial(_dwconv_kernel, bt=bt, zero_first=zero_first, rc=32),
        grid=(n_seq, nt, D_MODEL // bc),
        in_specs=[pl.BlockSpec((CONV_HALO, bc), prev_map),
                  pl.BlockSpec((bt, bc), lambda s, t, c: (s * nt + t, c)),
                  pl.BlockSpec((CONV_WIDTH, bc), lambda s, t, c: (0, c)),
                  pl.BlockSpec((1, bc), lambda s, t, c: (0, c))],
        out_specs=pl.BlockSpec((bt, bc), lambda s, t, c: (s * nt + t, c)),
        out_shape=jax.ShapeDtypeStruct(u.shape, F32),
        scratch_shapes=[pltpu.VMEM((SUBLANES, bt + CONV_HALO, bc), F32)],
        compiler_params=_params(3, 32),
    )(prev_src, u, w, b)


def _two_group_specs(bm, width, n_first):
    nb = n_first // bm
    return (pl.BlockSpec((bm, width), lambda i, *_: (jnp.minimum(i, nb - 1), 0)),
            pl.BlockSpec((bm, width), lambda i, *_: (jnp.maximum(i - nb, 0), 0)))


def _pick_group(first_ref, second_ref, n_first_blocks):
    return jnp.where(pl.program_id(0) < n_first_blocks, first_ref[...], second_ref[...])


def _pw2_ln1_kernel(yp_ref, ys_ref, xp_ref, xs_ref, w_ref, b2_ref, cg_ref, cb_ref, g_ref, b_ref, h_ref, hb_ref,
                    *, nb_first):
    y = _silu(_ln(_pick_group(yp_ref, ys_ref, nb_first), cg_ref[...], cb_ref[...]))
    mix = _dot(y.astype(BF16), w_ref[...]) + b2_ref[...]
    h = _ln(ALPHA * _pick_group(xp_ref, xs_ref, nb_first) + mix, g_ref[...], b_ref[...])
    h_ref[...] = h
    hb_ref[...] = h.astype(BF16)


def _pw2_ln1(y_p, y_s, x_p, x_s, w, b2, cg, cb, g, b):
    n_p, n = y_p.shape[0], y_p.shape[0] + y_s.shape[0]
    bm = 256
    row = pl.BlockSpec((bm, D_MODEL), lambda i: (i, 0))
    vec = pl.BlockSpec((1, D_MODEL), lambda i: (0, 0))
    return pl.pallas_call(
        functools.partial(_pw2_ln1_kernel, nb_first=n_p // bm),
        grid=(n // bm,),
        in_specs=[*_two_group_specs(bm, D_MODEL, n_p), *_two_group_specs(bm, D_MODEL, n_p),
                  pl.BlockSpec((D_MODEL, D_MODEL), lambda i: (0, 0)), vec, vec, vec, vec, vec],
        out_specs=[row, row],
        out_shape=[jax.ShapeDtypeStruct((n, D_MODEL), F32), jax.ShapeDtypeStruct((n, D_MODEL), BF16)],
        compiler_params=_params(1, 56),
    )(y_p, y_s, x_p, x_s, w, b2, cg, cb, g, b)


def _ffn_kernel(x_ref, wg_ref, wu_ref, wd_ref, o_ref):
    j = pl.program_id(1)
    x = x_ref[...]
    g = _dot(x, wg_ref[...])
    u = _dot(x, wu_ref[...])
    part = _dot((_silu(g) * u).astype(BF16), wd_ref[...])

    @pl.when(j == 0)
    def _():
        o_ref[...] = part

    @pl.when(j > 0)
    def _():
        o_ref[...] += part


def _ffn(xb, wg, wu, wd):
    n = xb.shape[0]
    bm, bf = math.gcd(n, 1024), 512
    return pl.pallas_call(
        _ffn_kernel,
        grid=(n // bm, D_FF // bf),
        in_specs=[pl.BlockSpec((bm, D_MODEL), lambda i, j: (i, 0)),
                  pl.BlockSpec((D_MODEL, bf), lambda i, j: (0, j)),
                  pl.BlockSpec((D_MODEL, bf), lambda i, j: (0, j)),
                  pl.BlockSpec((bf, D_MODEL), lambda i, j: (j, 0))],
        out_specs=pl.BlockSpec((bm, D_MODEL), lambda i, j: (i, 0)),
        out_shape=jax.ShapeDtypeStruct((n, D_MODEL), F32),
        compiler_params=_params(2, 60),
    )(xb, wg, wu, wd)


def _ple_ln2_kernel(h_ref, hb_ref, f_ref, pp_ref, ps_ref, wg_ref, wp_ref, g_ref, b_ref, x_ref, xb_ref, *, nb_first):
    gate = jax.nn.sigmoid(_dot(hb_ref[...], wg_ref[...]))
    proj = _dot(_pick_group(pp_ref, ps_ref, nb_first).astype(BF16), wp_ref[...])
    x = _ln(ALPHA * h_ref[...] + f_ref[...] + gate * proj, g_ref[...], b_ref[...])
    x_ref[...] = x
    xb_ref[...] = x.astype(BF16)


def _ple_ln2(h, hb, f, p_p, p_s, wg, wp, g, b):
    n = h.shape[0]
    bm = 256
    row = pl.BlockSpec((bm, D_MODEL), lambda i: (i, 0))
    vec = pl.BlockSpec((1, D_MODEL), lambda i: (0, 0))
    return pl.pallas_call(
        functools.partial(_ple_ln2_kernel, nb_first=p_p.shape[0] // bm),
        grid=(n // bm,),
        in_specs=[row, row, row, *_two_group_specs(bm, PLE_DIM, p_p.shape[0]),
                  pl.BlockSpec((D_MODEL, D_MODEL), lambda i: (0, 0)),
                  pl.BlockSpec((PLE_DIM, D_MODEL), lambda i: (0, 0)), vec, vec],
        out_specs=[row, row],
        out_shape=[jax.ShapeDtypeStruct((n, D_MODEL), F32), jax.ShapeDtypeStruct((n, D_MODEL), BF16)],
        compiler_params=_params(1, 48),
    )(h, hb, f, p_p, p_s, wg, wp, g, b)


def _mm_kernel(x_ref, w_ref, o_ref, *, scale):
    acc = _dot(x_ref[...], w_ref[...])
    if scale != 1.0:
        acc = acc * scale
    o_ref[...] = acc.astype(o_ref.dtype)


def _mm(xb, w, *, scale, out_dtype):
    n, k = xb.shape
    m = w.shape[1]
    bm, bn = math.gcd(n, 1024), 512
    return pl.pallas_call(
        functools.partial(_mm_kernel, scale=scale),
        grid=(n // bm, m // bn),
        in_specs=[pl.BlockSpec((bm, k), lambda i, j: (i, 0)),
                  pl.BlockSpec((k, bn), lambda i, j: (0, j))],
        out_specs=pl.BlockSpec((bm, bn), lambda i, j: (i, j)),
        out_shape=jax.ShapeDtypeStruct((n, m), out_dtype),
        compiler_params=_params(2, 32),
    )(xb, w)


def _t5_bucket(rel):
    half = NUM_BUCKETS // 2
    exact = half // 2
    ret = jnp.where(rel > 0, half, 0)
    n = jnp.abs(rel)
    large = exact + (jnp.log(jnp.maximum(n, 1).astype(F32) / exact)
                     / math.log(MAX_DISTANCE / exact) * (half - exact)).astype(jnp.int32)
    large = jnp.minimum(large, half - 1)
    return ret + jnp.where(n < exact, n, large)


def _bias_kernel(table_ref, bucket_ref, o_ref):
    bk = bucket_ref[...]
    masks = [bk == b for b in range(NUM_BUCKETS)]
    for h in range(N_HEADS):
        acc = jnp.zeros(bk.shape, F32)
        for b in range(NUM_BUCKETS):
            acc = jnp.where(masks[b], table_ref[b, h], acc)
        o_ref[h] = acc


def _rel_bias(table):
    i = jnp.arange(CHUNK)[:, None]
    j = jnp.arange(BAND)[None, :]
    bucket = _t5_bucket((j - WINDOW) - i).astype(jnp.int32)
    out = pl.pallas_call(
        _bias_kernel,
        in_specs=[pl.BlockSpec(memory_space=pltpu.SMEM), pl.BlockSpec(memory_space=pltpu.VMEM)],
        out_specs=pl.BlockSpec(memory_space=pltpu.VMEM),
        out_shape=jax.ShapeDtypeStruct((N_HEADS, CHUNK, BAND), F32),
    )(table.astype(F32), bucket)
    return out.reshape(N_KV_HEADS, GROUP * CHUNK, BAND)


def _attn_kernel(q_ref, kp_ref, vp_ref, kc_ref, vc_ref, bias_ref, sink_ref, o_ref, q_scr, k_scr, v_scr,
                 *, tq, mask_first):
    t = pl.program_id(1)
    for h in range(N_KV_HEADS):
        hs = slice(h * HEAD_DIM, (h + 1) * HEAD_DIM)
        k_scr[h, 0:WINDOW, :] = kp_ref[:, hs].astype(BF16)
        k_scr[h, WINDOW:, :] = kc_ref[:, hs].astype(BF16)
        v_scr[h, 0:WINDOW, :] = vp_ref[:, hs].astype(BF16)
        v_scr[h, WINDOW:, :] = vc_ref[:, hs].astype(BF16)
    for h in range(N_HEADS):
        q_scr[h] = q_ref[:, h * HEAD_DIM:(h + 1) * HEAD_DIM]
    rows = GROUP * CHUNK
    bias = bias_ref[...]
    sink = sink_ref[...]
    n_chunks = tq // CHUNK
    scores = []
    for c in range(n_chunks):
        cs = slice(c * CHUNK, (c + 1) * CHUNK)
        ks = slice(c * CHUNK, c * CHUNK + BAND)
        qh = q_scr[:, cs, :].reshape(N_KV_HEADS, rows, HEAD_DIM)
        scores.append(jnp.einsum('hqd,hkd->hqk', qh, k_scr[:, ks, :], preferred_element_type=F32))
    probs = []
    for c in range(n_chunks):
        s = scores[c] + bias
        if mask_first and c * CHUNK < WINDOW:
            first_valid = WINDOW - (t * tq + c * CHUNK)
            kidx = lax.broadcasted_iota(jnp.int32, s.shape, 2)
            s = jnp.where(kidx >= first_valid, s, NEG_INF)
        m = jnp.maximum(jnp.max(s, axis=-1, keepdims=True), sink)
        e = jnp.exp(s - m)
        den = jnp.sum(e, axis=-1, keepdims=True) + jnp.exp(sink - m)
        probs.append((e * (1.0 / den)).astype(BF16))
    for c in range(n_chunks):
        cs = slice(c * CHUNK, (c + 1) * CHUNK)
        ks = slice(c * CHUNK, c * CHUNK + BAND)
        o = jnp.einsum('hqk,hkd->hqd', probs[c], v_scr[:, ks, :], preferred_element_type=F32)
        o = o.reshape(N_HEADS, CHUNK, HEAD_DIM).astype(o_ref.dtype)
        for h in range(N_HEADS):
            o_ref[cs, h * HEAD_DIM:(h + 1) * HEAD_DIM] = o[h]


def _attention(q, kv, k_prev, v_prev, bias, sinks, *, n_seq, seq, tq, row0, mask_first):
    nt = seq // tq
    qb0 = row0 // tq
    if mask_first:
        wpt = tq // WINDOW
        kp_spec = pl.BlockSpec((WINDOW, KV_DIM), lambda s, t: (jnp.maximum((s * nt + t) * wpt - 1, 0), 0))
        vp_spec = pl.BlockSpec((WINDOW, KV_DIM), lambda s, t: (jnp.maximum((s * nt + t) * wpt - 1, 0), 1))
    else:
        kp_spec = pl.BlockSpec((WINDOW, KV_DIM), lambda s, t: (s, 0))
        vp_spec = pl.BlockSpec((WINDOW, KV_DIM), lambda s, t: (s, 0))
    return pl.pallas_call(
        functools.partial(_attn_kernel, tq=tq, mask_first=mask_first),
        grid=(n_seq, nt),
        in_specs=[pl.BlockSpec((tq, D_MODEL), lambda s, t: (qb0 + s * nt + t, 0)),
                  kp_spec, vp_spec,
                  pl.BlockSpec((tq, KV_DIM), lambda s, t: (qb0 + s * nt + t, 0)),
                  pl.BlockSpec((tq, KV_DIM), lambda s, t: (qb0 + s * nt + t, 1)),
                  pl.BlockSpec((N_KV_HEADS, GROUP * CHUNK, BAND), lambda s, t: (0, 0, 0)),
                  pl.BlockSpec((N_KV_HEADS, GROUP * CHUNK, 1), lambda s, t: (0, 0, 0))],
        out_specs=pl.BlockSpec((tq, D_MODEL), lambda s, t: (s * nt + t, 0)),
        out_shape=jax.ShapeDtypeStruct((n_seq * seq, D_MODEL), BF16),
        scratch_shapes=[pltpu.VMEM((N_HEADS, tq, HEAD_DIM), BF16),
                        pltpu.VMEM((N_KV_HEADS, WINDOW + tq, HEAD_DIM), BF16),
                        pltpu.VMEM((N_KV_HEADS, WINDOW + tq, HEAD_DIM), BF16)],
        compiler_params=_params(2, 40),
    )(q, k_prev, v_prev, kv, kv, bias, sinks)


def _f32_bits(x):
    return lax.bitcast_convert_type(x, jnp.uint32)


def _pack_bf16_pairs(x):
    c = x.shape[1] // 2
    lo = _f32_bits(x[:, :c].astype(BF16).astype(F32))
    hi = _f32_bits(x[:, c:].astype(BF16).astype(F32))
    return hi | (lo >> 16)


def _unpack_bf16_pairs(w):
    lo = lax.bitcast_convert_type(w << 16, F32).astype(BF16)
    hi = lax.bitcast_convert_type(w & jnp.uint32(0xFFFF0000), F32).astype(BF16)
    return lo, hi


def _route(h, wh, wl, b):
    hh = h.astype(BF16)
    hl = (h - hh.astype(F32)).astype(BF16)
    logits = _dot(hh, wh) + (_dot(hl, wh) + _dot(hh, wl)) + b
    lane = lax.broadcasted_iota(jnp.int32, logits.shape, 1).astype(F32)
    ninf = -jnp.inf
    lg = jnp.where(lane < N_EXPERTS, logits, ninf)
    m1 = jnp.max(lg, axis=-1, keepdims=True)
    i1 = jnp.min(jnp.where(lg == m1, lane, float(LANES)), axis=-1, keepdims=True)
    lg2 = jnp.where(lane == i1, ninf, lg)
    m2 = jnp.max(lg2, axis=-1, keepdims=True)
    i2 = jnp.min(jnp.where(lg2 == m2, lane, float(LANES)), axis=-1, keepdims=True)
    ex = jnp.exp(m2 - m1)
    den = 1.0 + ex
    out = jnp.where(lane == 0, i1, jnp.where(lane == 1, i2, jnp.where(lane == 2, 1.0 / den, ex / den)))
    return jnp.where(lane < 4, out, 0.0)


def _oproj_ln1_kernel(op_ref, os_ref, x_ref, w_ref, g_ref, b_ref, rwh_ref, rwl_ref, rb_ref, h_ref, hp_ref, r_ref,
                      *, nb_first):
    mix = _dot(_pick_group(op_ref, os_ref, nb_first), w_ref[...])
    h = _ln(ALPHA * x_ref[...] + mix, g_ref[...], b_ref[...])
    h_ref[...] = h
    r_ref[...] = _route(h, rwh_ref[...], rwl_ref[...], rb_ref[...])
    packed = _pack_bf16_pairs(h)
    for j in range(PACKED_TILES):
        hp_ref[pl.ds(j, h.shape[0], stride=PACKED_TILES), :] = packed[:, j * LANES:(j + 1) * LANES]


def _oproj_ln1_route(o_p, o_s, x, w, g, b, w_router, b_router):
    n = x.shape[0]
    bm = 256
    row = pl.BlockSpec((bm, D_MODEL), lambda i: (i, 0))
    vec = pl.BlockSpec((1, D_MODEL), lambda i: (0, 0))
    rw = pl.BlockSpec((D_MODEL, LANES), lambda i: (0, 0))
    wp = jnp.zeros((D_MODEL, LANES), F32).at[:, :N_EXPERTS].set(w_router.astype(F32))
    wh = wp.astype(BF16)
    wl = (wp - wh.astype(F32)).astype(BF16)
    bp = jnp.zeros((1, LANES), F32).at[0, :N_EXPERTS].set(b_router.astype(F32))
    return pl.pallas_call(
        functools.partial(_oproj_ln1_kernel, nb_first=o_p.shape[0] // bm),
        grid=(n // bm,),
        in_specs=[*_two_group_specs(bm, D_MODEL, o_p.shape[0]), row,
                  pl.BlockSpec((D_MODEL, D_MODEL), lambda i: (0, 0)), vec, vec,
                  rw, rw, pl.BlockSpec((1, LANES), lambda i: (0, 0))],
        out_specs=[row, pl.BlockSpec((bm * PACKED_TILES, LANES), lambda i: (i, 0)),
                   pl.BlockSpec((bm, LANES), lambda i: (i, 0))],
        out_shape=[jax.ShapeDtypeStruct((n, D_MODEL), F32),
                   jax.ShapeDtypeStruct((n * PACKED_TILES, LANES), jnp.uint32),
                   jax.ShapeDtypeStruct((n, LANES), F32)],
        compiler_params=_params(1, 48),
    )(o_p, o_s, x, w, g, b, wh, wl, bp)


def _gather_kernel(tok_ref, h_hbm, o_ref, buf, sem, *, rows, n_blocks):
    i = pl.program_id(0)

    def row_copy(tok, slot, r):
        src = pl.multiple_of(tok * PACKED_TILES, PACKED_TILES)
        dst = pl.multiple_of(r * PACKED_TILES, PACKED_TILES)
        return pltpu.make_async_copy(h_hbm.at[pl.ds(src, PACKED_TILES)], buf.at[slot, pl.ds(dst, PACKED_TILES)],
                                     sem.at[slot])

    def issue(blk, slot):
        def body(r8, c):
            for q in range(DMA_ISSUE_UNROLL):
                r = r8 * DMA_ISSUE_UNROLL + q
                row_copy(tok_ref[blk * rows + r], slot, r).start(priority=q % DMA_PRIORITIES)
            return c
        lax.fori_loop(0, rows // DMA_ISSUE_UNROLL, body, 0)

    @pl.when(i == 0)
    def _():
        issue(0, 0)

    @pl.when(i + 1 < n_blocks)
    def _():
        issue(i + 1, (i + 1) % 2)

    slot = i % 2
    pltpu.make_async_copy(h_hbm.at[pl.ds(0, rows * PACKED_TILES)], buf.at[slot], sem.at[slot]).wait()
    half = D_MODEL // 2
    for j in range(PACKED_TILES):
        lo, hi = _unpack_bf16_pairs(buf[slot, pl.ds(j, rows, stride=PACKED_TILES), :])
        o_ref[:, j * LANES:(j + 1) * LANES] = lo
        o_ref[:, half + j * LANES:half + (j + 1) * LANES] = hi


def _gather_rows(slot_tok, hp):
    cap = slot_tok.shape[0]
    rows = MOE_SUB_ROWS
    n_blocks = cap // rows
    return pl.pallas_call(
        functools.partial(_gather_kernel, rows=rows, n_blocks=n_blocks),
        grid_spec=pltpu.PrefetchScalarGridSpec(
            num_scalar_prefetch=1,
            grid=(n_blocks,),
            in_specs=[pl.BlockSpec(memory_space=pl.ANY)],
            out_specs=pl.BlockSpec((rows, D_MODEL), lambda i, tok: (i, 0)),
            scratch_shapes=[pltpu.VMEM((2, rows * PACKED_TILES, LANES), jnp.uint32),
                            pltpu.SemaphoreType.DMA((2,))]),
        out_shape=jax.ShapeDtypeStruct((cap, D_MODEL), BF16),
        compiler_params=_params(1, 32),
    )(slot_tok, hp)


def _expert_changed(be_ref, i):
    return jnp.logical_or(i == 0, be_ref[i] != be_ref[jnp.maximum(i - 1, 0)])


def _moe_up_kernel(be_ref, nu_ref, x_ref, wg_ref, wu_ref, a_ref, wgb_ref, wub_ref):
    i = pl.program_id(1)
    used = i < nu_ref[0]

    @pl.when(jnp.logical_and(used, _expert_changed(be_ref, i)))
    def _():
        wgb_ref[...] = wg_ref[...].astype(BF16)
        wub_ref[...] = wu_ref[...].astype(BF16)

    @pl.when(used)
    def _():
        x = x_ref[...]
        g = _dot(x, wgb_ref[...])
        u = _dot(x, wub_ref[...])
        a_ref[...] = (_silu(g) * u).astype(a_ref.dtype)

    @pl.when(jnp.logical_not(used))
    def _():
        a_ref[...] = jnp.zeros(a_ref.shape, a_ref.dtype)


def _moe_up(block_e, n_used, xs, wg, wu):
    cap = xs.shape[0]
    bf = 512
    return pl.pallas_call(
        _moe_up_kernel,
        grid_spec=pltpu.PrefetchScalarGridSpec(
            num_scalar_prefetch=2,
            grid=(D_FF // bf, cap // MOE_ROWS),
            in_specs=[pl.BlockSpec((MOE_ROWS, D_MODEL), lambda j, i, be, nu: (i, 0)),
                      pl.BlockSpec((None, D_MODEL, bf), lambda j, i, be, nu: (be[i], 0, j)),
                      pl.BlockSpec((None, D_MODEL, bf), lambda j, i, be, nu: (be[i], 0, j))],
            out_specs=pl.BlockSpec((MOE_ROWS, bf), lambda j, i, be, nu: (i, j)),
            scratch_shapes=[pltpu.VMEM((D_MODEL, bf), BF16), pltpu.VMEM((D_MODEL, bf), BF16)]),
        out_shape=jax.ShapeDtypeStruct((cap, D_FF), BF16),
        compiler_params=_params(2, 48),
    )(block_e, n_used, xs, wg, wu)


def _moe_down_kernel(be_ref, nu_ref, a_ref, wd_ref, y_ref, wdb_ref):
    i = pl.program_id(1)
    used = i < nu_ref[0]

    @pl.when(jnp.logical_and(used, _expert_changed(be_ref, i)))
    def _():
        wdb_ref[...] = wd_ref[...].astype(BF16)

    @pl.when(used)
    def _():
        y_ref[...] = _dot(a_ref[...], wdb_ref[...])

    @pl.when(jnp.logical_not(used))
    def _():
        y_ref[...] = jnp.zeros(y_ref.shape, y_ref.dtype)


def _moe_down(block_e, n_used, a, wd):
    cap = a.shape[0]
    bn = 512
    return pl.pallas_call(
        _moe_down_kernel,
        grid_spec=pltpu.PrefetchScalarGridSpec(
            num_scalar_prefetch=2,
            grid=(D_MODEL // bn, cap // MOE_SUB_ROWS),
            in_specs=[pl.BlockSpec((MOE_SUB_ROWS, D_FF), lambda j, i, be, nu: (i, 0)),
                      pl.BlockSpec((None, D_FF, bn), lambda j, i, be, nu: (be[i], 0, j))],
            out_specs=pl.BlockSpec((MOE_SUB_ROWS, bn), lambda j, i, be, nu: (i, j)),
            scratch_shapes=[pltpu.VMEM((D_FF, bn), BF16)]),
        out_shape=jax.ShapeDtypeStruct((cap, D_MODEL), F32),
        compiler_params=_params(2, 56),
    )(block_e, n_used, a, wd)


def _combine_kernel(dest_ref, h_ref, r_ref, p_ref, wg_ref, wp_ref, g_ref, b_ref, ys_hbm, y_ref, buf, sem,
                    *, bm, row0, n_blocks):
    i = pl.program_id(0)

    def row_copy(src, slot, k, r):
        return pltpu.make_async_copy(ys_hbm.at[pl.ds(src, 1)], buf.at[slot, k, pl.ds(r, 1)], sem.at[slot])

    def issue(blk, slot):
        def body(r4, c):
            for q in range(DMA_ISSUE_UNROLL // TOP_K):
                r = r4 * (DMA_ISSUE_UNROLL // TOP_K) + q
                a = TOP_K * (row0 + blk * bm + r)
                for k in range(TOP_K):
                    row_copy(dest_ref[a + k], slot, k, r).start(priority=k % DMA_PRIORITIES)
            return c
        lax.fori_loop(0, bm // (DMA_ISSUE_UNROLL // TOP_K), body, 0)

    @pl.when(i == 0)
    def _():
        issue(0, 0)

    @pl.when(i + 1 < n_blocks)
    def _():
        issue(i + 1, (i + 1) % 2)

    slot = i % 2
    for k in range(TOP_K):
        pltpu.make_async_copy(ys_hbm.at[pl.ds(0, bm)], buf.at[slot, k], sem.at[slot]).wait()

    h = h_ref[...]
    r = r_ref[...]
    f = buf[slot, 0] * r[:, 2:3] + buf[slot, 1] * r[:, 3:4]
    gate = jax.nn.sigmoid(_dot(h.astype(BF16), wg_ref[...]))
    proj = _dot(p_ref[...].astype(BF16), wp_ref[...])
    y_ref[...] = _ln(ALPHA * h + f + gate * proj, g_ref[...], b_ref[...])


def _combine_ple_ln2(dest, h, route, p, ys, wg, wp, g, b, *, row0, n_rows):
    bm = 256
    n_blocks = n_rows // bm
    b0 = row0 // bm
    vec = pl.BlockSpec((1, D_MODEL), lambda i, d: (0, 0))
    return pl.pallas_call(
        functools.partial(_combine_kernel, bm=bm, row0=row0, n_blocks=n_blocks),
        grid_spec=pltpu.PrefetchScalarGridSpec(
            num_scalar_prefetch=1,
            grid=(n_blocks,),
            in_specs=[pl.BlockSpec((bm, D_MODEL), lambda i, d: (b0 + i, 0)),
                      pl.BlockSpec((bm, LANES), lambda i, d: (b0 + i, 0)),
                      pl.BlockSpec((bm, PLE_DIM), lambda i, d: (i, 0)),
                      pl.BlockSpec((D_MODEL, D_MODEL), lambda i, d: (0, 0)),
                      pl.BlockSpec((PLE_DIM, D_MODEL), lambda i, d: (0, 0)),
                      vec, vec,
                      pl.BlockSpec(memory_space=pl.ANY)],
            out_specs=pl.BlockSpec((bm, D_MODEL), lambda i, d: (i, 0)),
            scratch_shapes=[pltpu.VMEM((2, TOP_K, bm, D_MODEL), F32), pltpu.SemaphoreType.DMA((2,))]),
        out_shape=jax.ShapeDtypeStruct((n_rows, D_MODEL), F32),
        compiler_params=_params(1, 48),
    )(dest, h, route, p, wg, wp, g, b, ys)


def _route_plan(route):
    n = route.shape[0]
    n_assign = n * TOP_K
    flat_e = route[:, :TOP_K].astype(jnp.int32).reshape(n_assign)
    onehot = (flat_e[:, None] == jnp.arange(N_EXPERTS, dtype=jnp.int32)[None, :]).astype(jnp.int32)
    csum = jnp.cumsum(onehot, axis=0)
    counts = csum[-1]
    rank = jnp.sum(csum * onehot, axis=1) - 1
    padded = (counts + MOE_ROWS - 1) // MOE_ROWS * MOE_ROWS
    pad_end = jnp.cumsum(padded)
    pad_start = jnp.sum(onehot * (pad_end - padded)[None, :], axis=1)
    dest = (pad_start + rank).astype(jnp.int32)
    n_blocks = -(-n_assign // MOE_ROWS) + N_EXPERTS - 1
    cap = n_blocks * MOE_ROWS
    slot_tok = (jnp.arange(cap, dtype=jnp.int32) % n).at[dest].set(jnp.arange(n_assign, dtype=jnp.int32) // TOP_K)
    block_start = jnp.arange(n_blocks, dtype=pad_end.dtype) * MOE_ROWS
    block_e = jnp.minimum(jnp.sum(pad_end[None, :] <= block_start[:, None], axis=1), N_EXPERTS - 1).astype(jnp.int32)
    n_used = (pad_end[-1:] // MOE_ROWS).astype(jnp.int32)
    sub = MOE_ROWS // MOE_SUB_ROWS
    return dest, slot_tok, block_e, n_used, jnp.repeat(block_e, sub), n_used * sub


def kernel(x_prompt, x_sample, state_conv, cache_k, cache_v, p_prompt, p_sample, conv_w_pw1, conv_b_pw1, conv_w_dw, conv_b_dw, conv_ln_g, conv_ln_b, conv_w_pw2, conv_b_pw2, w_kv, rel_bias, attn_w_q, attn_sinks, attn_w_o, ln1_g, ln1_b, ln2_g, ln2_b, ffn_w_gate, ffn_w_up, ffn_w_down, moe_w_router, moe_b_router, moe_w_gate, moe_w_up, moe_w_down, ple_w_gate, ple_w_proj):
    n_pb, p_seq, _ = x_prompt.shape
    n_sb, s_seq, _ = x_sample.shape
    n_p, n_s = n_pb * p_seq, n_sb * s_seq
    vec = lambda v: v.reshape(1, -1).astype(F32)

    xp = x_prompt.reshape(n_p, D_MODEL)
    xs = x_sample.reshape(n_s, D_MODEL)
    w1 = conv_w_pw1[0].astype(BF16)
    b1 = vec(conv_b_pw1[0])
    u_p = _pw1_glu(xp, w1, b1)
    u_s = _pw1_glu(xs, w1, b1)
    w_dw, b_dw = conv_w_dw[0].astype(F32), vec(conv_b_dw[0])
    state = jnp.pad(state_conv[0], ((0, 0), (CONV_HALO - (CONV_WIDTH - 1), 0), (0, 0))).reshape(n_sb * CONV_HALO, D_MODEL)
    yc_p = _dwconv(u_p, u_p, w_dw, b_dw, n_seq=n_pb, seq=p_seq, bt=256, zero_first=True)
    yc_s = _dwconv(u_s, state, w_dw, b_dw, n_seq=n_sb, seq=s_seq, bt=s_seq, zero_first=False)
    w2 = conv_w_pw2[0].astype(BF16)
    tail0 = (w2, vec(conv_b_pw2[0]), vec(conv_ln_g[0]), vec(conv_ln_b[0]), vec(ln1_g[0]), vec(ln1_b[0]))
    h, hb = _pw2_ln1(yc_p, yc_s, xp, xs, *tail0)
    f = _ffn(hb, ffn_w_gate[0].astype(BF16), ffn_w_up[0].astype(BF16), ffn_w_down[0].astype(BF16))
    x1, x1b = _ple_ln2(h, hb, f, p_prompt[0].reshape(n_p, PLE_DIM), p_sample[0].reshape(n_s, PLE_DIM),
                       ple_w_gate[0].astype(BF16), ple_w_proj[0].astype(BF16), vec(ln2_g[0]), vec(ln2_b[0]))

    q = _mm(x1b, attn_w_q[0].astype(BF16), scale=HEAD_DIM ** -0.5, out_dtype=BF16)
    kv = _mm(x1b, w_kv.astype(BF16), scale=1.0, out_dtype=F32)
    bias = _rel_bias(rel_bias)
    sinks = jnp.repeat(attn_sinks[0].astype(F32), CHUNK).reshape(N_KV_HEADS, GROUP * CHUNK, 1)
    o_p = _attention(q, kv, kv, kv, bias, sinks, n_seq=n_pb, seq=p_seq, tq=256, row0=0, mask_first=True)
    o_s = _attention(q, kv, cache_k.reshape(n_sb * WINDOW, KV_DIM), cache_v.reshape(n_sb * WINDOW, KV_DIM),
                     bias, sinks, n_seq=n_sb, seq=s_seq, tq=s_seq, row0=n_p, mask_first=False)
    h1, h1p, route = _oproj_ln1_route(o_p, o_s, x1, attn_w_o[0].astype(BF16), vec(ln1_g[1]), vec(ln1_b[1]),
                                      moe_w_router[0], moe_b_router[0])

    dest, slot_tok, block_e, n_used, sub_block_e, sub_n_used = _route_plan(route)
    xg = _gather_rows(slot_tok, h1p)
    a = _moe_up(block_e, n_used, xg, moe_w_gate[0], moe_w_up[0])
    ys = _moe_down(sub_block_e, sub_n_used, a, moe_w_down[0])
    tail1 = (ple_w_gate[1].astype(BF16), ple_w_proj[1].astype(BF16), vec(ln2_g[1]), vec(ln2_b[1]))
    y_p = _combine_ple_ln2(dest, h1, route, p_prompt[1].reshape(n_p, PLE_DIM), ys, *tail1, row0=0, n_rows=n_p)
    y_s = _combine_ple_ln2(dest, h1, route, p_sample[1].reshape(n_s, PLE_DIM), ys, *tail1, row0=n_p, n_rows=n_s)

    keep = CONV_WIDTH - 1
    conv_p = u_p.reshape(n_pb, p_seq, D_MODEL)[None, :, p_seq - keep:]
    conv_s = u_s.reshape(n_sb, s_seq, D_MODEL)[None, :, s_seq - keep:]
    wps = p_seq // WINDOW
    kv_p = kv.reshape(-1, WINDOW, 2 * KV_DIM)[wps - 1:n_pb * wps:wps].reshape(n_pb, WINDOW, 2, N_KV_HEADS, HEAD_DIM)
    kv_s = kv[n_p:].reshape(n_sb, s_seq, 2, N_KV_HEADS, HEAD_DIM)
    k_s = jnp.concatenate([cache_k, kv_s[:, :, 0]], axis=1)[:, s_seq:]
    v_s = jnp.concatenate([cache_v, kv_s[:, :, 1]], axis=1)[:, s_seq:]
    return (y_p.reshape(n_pb, p_seq, D_MODEL), y_s.reshape(n_sb, s_seq, D_MODEL), conv_p, conv_s,
            kv_p[:, :, 0], kv_p[:, :, 1], k_s, v_s)
```

```python
import functools
import math

import jax
import jax.numpy as jnp
from jax import lax
from jax.experimental import pallas as pl
from jax.experimental.pallas import tpu as pltpu

D_MODEL = 2048
CHUNK = 64
CONV_WIDTH = 31
HEAD_DIM = 64
N_HEADS = 32
N_KV_HEADS = 8
GROUP = N_HEADS // N_KV_HEADS
WINDOW = 128
BAND = WINDOW + CHUNK
NUM_BUCKETS = 32
MAX_DISTANCE = 128
D_FF = 5632
N_EXPERTS = 8
TOP_K = 2
PLE_DIM = 256
DEPTH = 2
ALPHA = (2 * DEPTH) ** 0.25
LN_EPS = 1e-5
NEG_INF = -1e30

KV_DIM = N_KV_HEADS * HEAD_DIM
CONV_HALO = 32
MOE_ROWS = 1024
MOE_SUB_ROWS = 512
LANES = 128
SUBLANES = 8
PACKED_TILES = D_MODEL // 2 // LANES
DMA_ISSUE_UNROLL = 8
DMA_PRIORITIES = 2
MIB = 1024 * 1024

F32 = jnp.float32
BF16 = jnp.bfloat16


def _params(n_axes, vmem_mib):
    return pltpu.CompilerParams(dimension_semantics=("arbitrary",) * n_axes,
                                vmem_limit_bytes=vmem_mib * MIB)


def _dot(a, b):
    return jnp.dot(a, b, preferred_element_type=F32)


def _ln(x, g, b):
    mu = jnp.mean(x, axis=-1, keepdims=True)
    xc = x - mu
    var = jnp.mean(xc * xc, axis=-1, keepdims=True)
    return xc * lax.rsqrt(var + LN_EPS) * g + b


def _silu(x):
    return x * jax.nn.sigmoid(x)


def _pw1_glu_kernel(x_ref, wa_ref, wb_ref, ba_ref, bb_ref, u_ref, xb_ref):
    @pl.when(pl.program_id(1) == 0)
    def _():
        xb_ref[...] = x_ref[...].astype(BF16)

    xb = xb_ref[...]
    a = _dot(xb, wa_ref[...]) + ba_ref[...]
    g = _dot(xb, wb_ref[...]) + bb_ref[...]
    u_ref[...] = a * jax.nn.sigmoid(g)


def _pw1_glu(x, w, b):
    n = x.shape[0]
    bm, bn = math.gcd(n, 1024), 512
    nj = D_MODEL // bn
    return pl.pallas_call(
        _pw1_glu_kernel,
        grid=(n // bm, nj),
        in_specs=[pl.BlockSpec((bm, D_MODEL), lambda i, j: (i, 0)),
                  pl.BlockSpec((D_MODEL, bn), lambda i, j: (0, j)),
                  pl.BlockSpec((D_MODEL, bn), lambda i, j: (0, j + nj)),
                  pl.BlockSpec((1, bn), lambda i, j: (0, j)),
                  pl.BlockSpec((1, bn), lambda i, j: (0, j + nj))],
        out_specs=pl.BlockSpec((bm, bn), lambda i, j: (i, j)),
        out_shape=jax.ShapeDtypeStruct((n, D_MODEL), F32),
        scratch_shapes=[pltpu.VMEM((bm, D_MODEL), BF16)],
        compiler_params=_params(2, 48),
    )(x, w, w, b, b)


def _dwconv_kernel(prev_ref, cur_ref, w_ref, b_ref, y_ref, e_ref, *, bt, zero_first, rc):
    prev = prev_ref[...]
    if zero_first:
        prev = jnp.where(pl.program_id(1) == 0, 0.0, prev)
    e_ref[0, 0:CONV_HALO, :] = prev
    e_ref[0, CONV_HALO:, :] = cur_ref[...]
    e0 = e_ref[0]
    n_e = bt + CONV_HALO
    for rho in range(1, SUBLANES):
        e_ref[rho] = pltpu.roll(e0, n_e - rho, axis=0)
    off = CONV_HALO - (CONV_WIDTH - 1)
    for r0 in range(0, bt, rc):
        acc = jnp.broadcast_to(b_ref[...], (rc, b_ref.shape[1]))
        for k in range(CONV_WIDTH):
            rho = (k + off) % SUBLANES
            base = r0 + k + off - rho
            acc = acc + e_ref[rho, base:base + rc, :] * w_ref[k:k + 1, :]
        y_ref[r0:r0 + rc, :] = acc


def _dwconv(u, prev_src, w, b, *, n_seq, seq, bt, zero_first):
    bc = 512
    nt = seq // bt
    hb = bt // CONV_HALO
    if zero_first:
        prev_map = lambda s, t, c: (jnp.maximum((s * nt + t) * hb - 1, 0), c)
    else:
        prev_map = lambda s, t, c: (s, c)
    return pl.pallas_call(
        functools.partial(_dwconv_kernel, bt=bt, zero_first=zero_first, rc=32),
        grid=(n_seq, nt, D_MODEL // bc),
        in_specs=[pl.BlockSpec((CONV_HALO, bc), prev_map),
                  pl.BlockSpec((bt, bc), lambda s, t, c: (s * nt + t, c)),
                  pl.BlockSpec((CONV_WIDTH, bc), lambda s, t, c: (0, c)),
                  pl.BlockSpec((1, bc), lambda s, t, c: (0, c))],
        out_specs=pl.BlockSpec((bt, bc), lambda s, t, c: (s * nt + t, c)),
        out_shape=jax.ShapeDtypeStruct(u.shape, F32),
        scratch_shapes=[pltpu.VMEM((SUBLANES, bt + CONV_HALO, bc), F32)],
        compiler_params=_params(3, 32),
    )(prev_src, u, w, b)


def _two_group_specs(bm, width, n_first):
    nb = n_first // bm
    return (pl.BlockSpec((bm, width), lambda i, *_: (jnp.minimum(i, nb - 1), 0)),
            pl.BlockSpec((bm, width), lambda i, *_: (jnp.maximum(i - nb, 0), 0)))


def _pick_group(first_ref, second_ref, n_first_blocks):
    return jnp.where(pl.program_id(0) < n_first_blocks, first_ref[...], second_ref[...])


def _pw2_ln1_kernel(yp_ref, ys_ref, xp_ref, xs_ref, w_ref, b2_ref, cg_ref, cb_ref, g_ref, b_ref, h_ref, hb_ref,
                    *, nb_first):
    y = _silu(_ln(_pick_group(yp_ref, ys_ref, nb_first), cg_ref[...], cb_ref[...]))
    mix = _dot(y.astype(BF16), w_ref[...]) + b2_ref[...]
    h = _ln(ALPHA * _pick_group(xp_ref, xs_ref, nb_first) + mix, g_ref[...], b_ref[...])
    h_ref[...] = h
    hb_ref[...] = h.astype(BF16)


def _pw2_ln1(y_p, y_s, x_p, x_s, w, b2, cg, cb, g, b):
    n_p, n = y_p.shape[0], y_p.shape[0] + y_s.shape[0]
    bm = 256
    row = pl.BlockSpec((bm, D_MODEL), lambda i: (i, 0))
    vec = pl.BlockSpec((1, D_MODEL), lambda i: (0, 0))
    return pl.pallas_call(
        functools.partial(_pw2_ln1_kernel, nb_first=n_p // bm),
        grid=(n // bm,),
        in_specs=[*_two_group_specs(bm, D_MODEL, n_p), *_two_group_specs(bm, D_MODEL, n_p),
                  pl.BlockSpec((D_MODEL, D_MODEL), lambda i: (0, 0)), vec, vec, vec, vec, vec],
        out_specs=[row, row],
        out_shape=[jax.ShapeDtypeStruct((n, D_MODEL), F32), jax.ShapeDtypeStruct((n, D_MODEL), BF16)],
        compiler_params=_params(1, 56),
    )(y_p, y_s, x_p, x_s, w, b2, cg, cb, g, b)


def _ffn_kernel(x_ref, wg_ref, wu_ref, wd_ref, o_ref):
    j = pl.program_id(1)
    x = x_ref[...]
    g = _dot(x, wg_ref[...])
    u = _dot(x, wu_ref[...])
    part = _dot((_silu(g) * u).astype(BF16), wd_ref[...])

    @pl.when(j == 0)
    def _():
        o_ref[...] = part

    @pl.when(j > 0)
    def _():
        o_ref[...] += part


def _ffn(xb, wg, wu, wd):
    n = xb.shape[0]
    bm, bf = math.gcd(n, 1024), 512
    return pl.pallas_call(
        _ffn_kernel,
        grid=(n // bm, D_FF // bf),
        in_specs=[pl.BlockSpec((bm, D_MODEL), lambda i, j: (i, 0)),
                  pl.BlockSpec((D_MODEL, bf), lambda i, j: (0, j)),
                  pl.BlockSpec((D_MODEL, bf), lambda i, j: (0, j)),
                  pl.BlockSpec((bf, D_MODEL), lambda i, j: (j, 0))],
        out_specs=pl.BlockSpec((bm, D_MODEL), lambda i, j: (i, 0)),
        out_shape=jax.ShapeDtypeStruct((n, D_MODEL), F32),
        compiler_params=_params(2, 60),
    )(xb, wg, wu, wd)


def _ple_ln2_kernel(h_ref, hb_ref, f_ref, pp_ref, ps_ref, wg_ref, wp_ref, g_ref, b_ref, x_ref, xb_ref, *, nb_first):
    gate = jax.nn.sigmoid(_dot(hb_ref[...], wg_ref[...]))
    proj = _dot(_pick_group(pp_ref, ps_ref, nb_first).astype(BF16), wp_ref[...])
    x = _ln(ALPHA * h_ref[...] + f_ref[...] + gate * proj, g_ref[...], b_ref[...])
    x_ref[...] = x
    xb_ref[...] = x.astype(BF16)


def _ple_ln2(h, hb, f, p_p, p_s, wg, wp, g, b):
    n = h.shape[0]
    bm = 256
    row = pl.BlockSpec((bm, D_MODEL), lambda i: (i, 0))
    vec = pl.BlockSpec((1, D_MODEL), lambda i: (0, 0))
    return pl.pallas_call(
        functools.partial(_ple_ln2_kernel, nb_first=p_p.shape[0] // bm),
        grid=(n // bm,),
        in_specs=[row, row, row, *_two_group_specs(bm, PLE_DIM, p_p.shape[0]),
                  pl.BlockSpec((D_MODEL, D_MODEL), lambda i: (0, 0)),
                  pl.BlockSpec((PLE_DIM, D_MODEL), lambda i: (0, 0)), vec, vec],
        out_specs=[row, row],
        out_shape=[jax.ShapeDtypeStruct((n, D_MODEL), F32), jax.ShapeDtypeStruct((n, D_MODEL), BF16)],
        compiler_params=_params(1, 48),
    )(h, hb, f, p_p, p_s, wg, wp, g, b)


def _mm_kernel(x_ref, w_ref, o_ref, *, scale):
    acc = _dot(x_ref[...], w_ref[...])
    if scale != 1.0:
        acc = acc * scale
    o_ref[...] = acc.astype(o_ref.dtype)


def _mm(xb, w, *, scale, out_dtype):
    n, k = xb.shape
    m = w.shape[1]
    bm, bn = math.gcd(n, 1024), 512
    return pl.pallas_call(
        functools.partial(_mm_kernel, scale=scale),
        grid=(n // bm, m // bn),
        in_specs=[pl.BlockSpec((bm, k), lambda i, j: (i, 0)),
                  pl.BlockSpec((k, bn), lambda i, j: (0, j))],
        out_specs=pl.BlockSpec((bm, bn), lambda i, j: (i, j)),
        out_shape=jax.ShapeDtypeStruct((n, m), out_dtype),
        compiler_params=_params(2, 32),
    )(xb, w)


def _t5_bucket(rel):
    half = NUM_BUCKETS // 2
    exact = half // 2
    ret = jnp.where(rel > 0, half, 0)
    n = jnp.abs(rel)
    large = exact + (jnp.log(jnp.maximum(n, 1).astype(F32) / exact)
                     / math.log(MAX_DISTANCE / exact) * (half - exact)).astype(jnp.int32)
    large = jnp.minimum(large, half - 1)
    return ret + jnp.where(n < exact, n, large)


def _bias_kernel(table_ref, bucket_ref, o_ref):
    bk = bucket_ref[...]
    masks = [bk == b for b in range(NUM_BUCKETS)]
    for h in range(N_HEADS):
        acc = jnp.zeros(bk.shape, F32)
        for b in range(NUM_BUCKETS):
            acc = jnp.where(masks[b], table_ref[b, h], acc)
        o_ref[h] = acc


def _rel_bias(table):
    i = jnp.arange(CHUNK)[:, None]
    j = jnp.arange(BAND)[None, :]
    bucket = _t5_bucket((j - WINDOW) - i).astype(jnp.int32)
    out = pl.pallas_call(
        _bias_kernel,
        in_specs=[pl.BlockSpec(memory_space=pltpu.SMEM), pl.BlockSpec(memory_space=pltpu.VMEM)],
        out_specs=pl.BlockSpec(memory_space=pltpu.VMEM),
        out_shape=jax.ShapeDtypeStruct((N_HEADS, CHUNK, BAND), F32),
    )(table.astype(F32), bucket)
    return out.reshape(N_KV_HEADS, GROUP * CHUNK, BAND)


def _attn_kernel(q_ref, kp_ref, vp_ref, kc_ref, vc_ref, bias_ref, sink_ref, o_ref, q_scr, k_scr, v_scr,
                 *, tq, mask_first):
    t = pl.program_id(1)
    for h in range(N_KV_HEADS):
        hs = slice(h * HEAD_DIM, (h + 1) * HEAD_DIM)
        k_scr[h, 0:WINDOW, :] = kp_ref[:, hs].astype(BF16)
        k_scr[h, WINDOW:, :] = kc_ref[:, hs].astype(BF16)
        v_scr[h, 0:WINDOW, :] = vp_ref[:, hs].astype(BF16)
        v_scr[h, WINDOW:, :] = vc_ref[:, hs].astype(BF16)
    for h in range(N_HEADS):
        q_scr[h] = q_ref[:, h * HEAD_DIM:(h + 1) * HEAD_DIM]
    rows = GROUP * CHUNK
    bias = bias_ref[...]
    sink = sink_ref[...]
    n_chunks = tq // CHUNK
    scores = []
    for c in range(n_chunks):
        cs = slice(c * CHUNK, (c + 1) * CHUNK)
        ks = slice(c * CHUNK, c * CHUNK + BAND)
        qh = q_scr[:, cs, :].reshape(N_KV_HEADS, rows, HEAD_DIM)
        scores.append(jnp.einsum('hqd,hkd->hqk', qh, k_scr[:, ks, :], preferred_element_type=F32))
    probs = []
    for c in range(n_chunks):
        s = scores[c] + bias
        if mask_first and c * CHUNK < WINDOW:
            first_valid = WINDOW - (t * tq + c * CHUNK)
            kidx = lax.broadcasted_iota(jnp.int32, s.shape, 2)
            s = jnp.where(kidx >= first_valid, s, NEG_INF)
        m = jnp.maximum(jnp.max(s, axis=-1, keepdims=True), sink)
        e = jnp.exp(s - m)
        den = jnp.sum(e, axis=-1, keepdims=True) + jnp.exp(sink - m)
        probs.append((e * (1.0 / den)).astype(BF16))
    for c in range(n_chunks):
        cs = slice(c * CHUNK, (c + 1) * CHUNK)
        ks = slice(c * CHUNK, c * CHUNK + BAND)
        o = jnp.einsum('hqk,hkd->hqd', probs[c], v_scr[:, ks, :], preferred_element_type=F32)
        o = o.reshape(N_HEADS, CHUNK, HEAD_DIM).astype(o_ref.dtype)
        for h in range(N_HEADS):
            o_ref[cs, h * HEAD_DIM:(h + 1) * HEAD_DIM] = o[h]


def _attention(q, kv, k_prev, v_prev, bias, sinks, *, n_seq, seq, tq, row0, mask_first):
    nt = seq // tq
    qb0 = row0 // tq
    if mask_first:
        wpt = tq // WINDOW
        kp_spec = pl.BlockSpec((WINDOW, KV_DIM), lambda s, t: (jnp.maximum((s * nt + t) * wpt - 1, 0), 0))
        vp_spec = pl.BlockSpec((WINDOW, KV_DIM), lambda s, t: (jnp.maximum((s * nt + t) * wpt - 1, 0), 1))
    else:
        kp_spec = pl.BlockSpec((WINDOW, KV_DIM), lambda s, t: (s, 0))
        vp_spec = pl.BlockSpec((WINDOW, KV_DIM), lambda s, t: (s, 0))
    return pl.pallas_call(
        functools.partial(_attn_kernel, tq=tq, mask_first=mask_first),
        grid=(n_seq, nt),
        in_specs=[pl.BlockSpec((tq, D_MODEL), lambda s, t: (qb0 + s * nt + t, 0)),
                  kp_spec, vp_spec,
                  pl.BlockSpec((tq, KV_DIM), lambda s, t: (qb0 + s * nt + t, 0)),
                  pl.BlockSpec((tq, KV_DIM), lambda s, t: (qb0 + s * nt + t, 1)),
                  pl.BlockSpec((N_KV_HEADS, GROUP * CHUNK, BAND), lambda s, t: (0, 0, 0)),
                  pl.BlockSpec((N_KV_HEADS, GROUP * CHUNK, 1), lambda s, t: (0, 0, 0))],
        out_specs=pl.BlockSpec((tq, D_MODEL), lambda s, t: (s * nt + t, 0)),
        out_shape=jax.ShapeDtypeStruct((n_seq * seq, D_MODEL), BF16),
        scratch_shapes=[pltpu.VMEM((N_HEADS, tq, HEAD_DIM), BF16),
                        pltpu.VMEM((N_KV_HEADS, WINDOW + tq, HEAD_DIM), BF16),
                        pltpu.VMEM((N_KV_HEADS, WINDOW + tq, HEAD_DIM), BF16)],
        compiler_params=_params(2, 40),
    )(q, k_prev, v_prev, kv, kv, bias, sinks)


def _f32_bits(x):
    return lax.bitcast_convert_type(x, jnp.uint32)


def _pack_bf16_pairs(x):
    c = x.shape[1] // 2
    lo = _f32_bits(x[:, :c].astype(BF16).astype(F32))
    hi = _f32_bits(x[:, c:].astype(BF16).astype(F32))
    return hi | (lo >> 16)


def _unpack_bf16_pairs(w):
    lo = lax.bitcast_convert_type(w << 16, F32).astype(BF16)
    hi = lax.bitcast_convert_type(w & jnp.uint32(0xFFFF0000), F32).astype(BF16)
    return lo, hi


def _route(h, wh, wl, b):
    hh = h.astype(BF16)
    hl = (h - hh.astype(F32)).astype(BF16)
    logits = _dot(hh, wh) + (_dot(hl, wh) + _dot(hh, wl)) + b
    lane = lax.broadcasted_iota(jnp.int32, logits.shape, 1).astype(F32)
    ninf = -jnp.inf
    lg = jnp.where(lane < N_EXPERTS, logits, ninf)
    m1 = jnp.max(lg, axis=-1, keepdims=True)
    i1 = jnp.min(jnp.where(lg == m1, lane, float(LANES)), axis=-1, keepdims=True)
    lg2 = jnp.where(lane == i1, ninf, lg)
    m2 = jnp.max(lg2, axis=-1, keepdims=True)
    i2 = jnp.min(jnp.where(lg2 == m2, lane, float(LANES)), axis=-1, keepdims=True)
    ex = jnp.exp(m2 - m1)
    den = 1.0 + ex
    out = jnp.where(lane == 0, i1, jnp.where(lane == 1, i2, jnp.where(lane == 2, 1.0 / den, ex / den)))
    return jnp.where(lane < 4, out, 0.0)


def _oproj_ln1_kernel(op_ref, os_ref, x_ref, w_ref, g_ref, b_ref, rwh_ref, rwl_ref, rb_ref, h_ref, hp_ref, r_ref,
                      *, nb_first):
    mix = _dot(_pick_group(op_ref, os_ref, nb_first), w_ref[...])
    h = _ln(ALPHA * x_ref[...] + mix, g_ref[...], b_ref[...])
    h_ref[...] = h
    r_ref[...] = _route(h, rwh_ref[...], rwl_ref[...], rb_ref[...])
    packed = _pack_bf16_pairs(h)
    for j in range(PACKED_TILES):
        hp_ref[pl.ds(j, h.shape[0], stride=PACKED_TILES), :] = packed[:, j * LANES:(j + 1) * LANES]


def _oproj_ln1_route(o_p, o_s, x, w, g, b, w_router, b_router):
    n = x.shape[0]
    bm = 256
    row = pl.BlockSpec((bm, D_MODEL), lambda i: (i, 0))
    vec = pl.BlockSpec((1, D_MODEL), lambda i: (0, 0))
    rw = pl.BlockSpec((D_MODEL, LANES), lambda i: (0, 0))
    wp = jnp.zeros((D_MODEL, LANES), F32).at[:, :N_EXPERTS].set(w_router.astype(F32))
    wh = wp.astype(BF16)
    wl = (wp - wh.astype(F32)).astype(BF16)
    bp = jnp.zeros((1, LANES), F32).at[0, :N_EXPERTS].set(b_router.astype(F32))
    return pl.pallas_call(
        functools.partial(_oproj_ln1_kernel, nb_first=o_p.shape[0] // bm),
        grid=(n // bm,),
        in_specs=[*_two_group_specs(bm, D_MODEL, o_p.shape[0]), row,
                  pl.BlockSpec((D_MODEL, D_MODEL), lambda i: (0, 0)), vec, vec,
                  rw, rw, pl.BlockSpec((1, LANES), lambda i: (0, 0))],
        out_specs=[row, pl.BlockSpec((bm * PACKED_TILES, LANES), lambda i: (i, 0)),
                   pl.BlockSpec((bm, LANES), lambda i: (i, 0))],
        out_shape=[jax.ShapeDtypeStruct((n, D_MODEL), F32),
                   jax.ShapeDtypeStruct((n * PACKED_TILES, LANES), jnp.uint32),
                   jax.ShapeDtypeStruct((n, LANES), F32)],
        compiler_params=_params(1, 48),
    )(o_p, o_s, x, w, g, b, wh, wl, bp)


def _gather_kernel(tok_ref, base_ref, h_hbm, o_ref, buf, sem, *, rows, n_blocks):
    i = pl.program_id(0)

    def row_copy(tok, slot, r):
        src = pl.multiple_of(tok * PACKED_TILES, PACKED_TILES)
        dst = pl.multiple_of(r * PACKED_TILES, PACKED_TILES)
        return pltpu.make_async_copy(h_hbm.at[pl.ds(src, PACKED_TILES)], buf.at[slot, pl.ds(dst, PACKED_TILES)],
                                     sem.at[slot])

    def issue(blk, slot):
        base = base_ref[blk]

        def body(r8, c):
            for q in range(DMA_ISSUE_UNROLL):
                r = r8 * DMA_ISSUE_UNROLL + q
                row_copy(tok_ref[base + r], slot, r).start(priority=q % DMA_PRIORITIES)
            return c
        lax.fori_loop(0, rows // DMA_ISSUE_UNROLL, body, 0)

    @pl.when(i == 0)
    def _():
        issue(0, 0)

    @pl.when(i + 1 < n_blocks)
    def _():
        issue(i + 1, (i + 1) % 2)

    slot = i % 2
    pltpu.make_async_copy(h_hbm.at[pl.ds(0, rows * PACKED_TILES)], buf.at[slot], sem.at[slot]).wait()
    half = D_MODEL // 2
    for j in range(PACKED_TILES):
        lo, hi = _unpack_bf16_pairs(buf[slot, pl.ds(j, rows, stride=PACKED_TILES), :])
        o_ref[:, j * LANES:(j + 1) * LANES] = lo
        o_ref[:, half + j * LANES:half + (j + 1) * LANES] = hi


def _gather_rows(sorted_tok, block_base, hp):
    rows = MOE_SUB_ROWS
    n_blocks = block_base.shape[0]
    return pl.pallas_call(
        functools.partial(_gather_kernel, rows=rows, n_blocks=n_blocks),
        grid_spec=pltpu.PrefetchScalarGridSpec(
            num_scalar_prefetch=2,
            grid=(n_blocks,),
            in_specs=[pl.BlockSpec(memory_space=pl.ANY)],
            out_specs=pl.BlockSpec((rows, D_MODEL), lambda i, tok, base: (i, 0)),
            scratch_shapes=[pltpu.VMEM((2, rows * PACKED_TILES, LANES), jnp.uint32),
                            pltpu.SemaphoreType.DMA((2,))]),
        out_shape=jax.ShapeDtypeStruct((n_blocks * rows, D_MODEL), BF16),
        compiler_params=_params(1, 32),
    )(sorted_tok, block_base, hp)


def _expert_changed(be_ref, i):
    return jnp.logical_or(i == 0, be_ref[i] != be_ref[jnp.maximum(i - 1, 0)])


def _moe_up_kernel(be_ref, nu_ref, x_ref, wg_ref, wu_ref, a_ref, wgb_ref, wub_ref):
    i = pl.program_id(1)
    used = i < nu_ref[0]

    @pl.when(jnp.logical_and(used, _expert_changed(be_ref, i)))
    def _():
        wgb_ref[...] = wg_ref[...].astype(BF16)
        wub_ref[...] = wu_ref[...].astype(BF16)

    @pl.when(used)
    def _():
        x = x_ref[...]
        g = _dot(x, wgb_ref[...])
        u = _dot(x, wub_ref[...])
        a_ref[...] = (_silu(g) * u).astype(a_ref.dtype)

    @pl.when(jnp.logical_not(used))
    def _():
        a_ref[...] = jnp.zeros(a_ref.shape, a_ref.dtype)


def _moe_up(block_e, n_used, xs, wg, wu):
    cap = xs.shape[0]
    bf = 512
    return pl.pallas_call(
        _moe_up_kernel,
        grid_spec=pltpu.PrefetchScalarGridSpec(
            num_scalar_prefetch=2,
            grid=(D_FF // bf, cap // MOE_ROWS),
            in_specs=[pl.BlockSpec((MOE_ROWS, D_MODEL), lambda j, i, be, nu: (i, 0)),
                      pl.BlockSpec((None, D_MODEL, bf), lambda j, i, be, nu: (be[i], 0, j)),
                      pl.BlockSpec((None, D_MODEL, bf), lambda j, i, be, nu: (be[i], 0, j))],
            out_specs=pl.BlockSpec((MOE_ROWS, bf), lambda j, i, be, nu: (i, j)),
            scratch_shapes=[pltpu.VMEM((D_MODEL, bf), BF16), pltpu.VMEM((D_MODEL, bf), BF16)]),
        out_shape=jax.ShapeDtypeStruct((cap, D_FF), BF16),
        compiler_params=_params(2, 48),
    )(block_e, n_used, xs, wg, wu)


def _moe_down_kernel(be_ref, nu_ref, a_ref, wd_ref, y_ref, wdb_ref):
    i = pl.program_id(1)
    used = i < nu_ref[0]

    @pl.when(jnp.logical_and(used, _expert_changed(be_ref, i)))
    def _():
        wdb_ref[...] = wd_ref[...].astype(BF16)

    @pl.when(used)
    def _():
        y_ref[...] = _dot(a_ref[...], wdb_ref[...])

    @pl.when(jnp.logical_not(used))
    def _():
        y_ref[...] = jnp.zeros(y_ref.shape, y_ref.dtype)


def _moe_down(block_e, n_used, a, wd):
    cap = a.shape[0]
    bn = 512
    return pl.pallas_call(
        _moe_down_kernel,
        grid_spec=pltpu.PrefetchScalarGridSpec(
            num_scalar_prefetch=2,
            grid=(D_MODEL // bn, cap // MOE_SUB_ROWS),
            in_specs=[pl.BlockSpec((MOE_SUB_ROWS, D_FF), lambda j, i, be, nu: (i, 0)),
                      pl.BlockSpec((None, D_FF, bn), lambda j, i, be, nu: (be[i], 0, j))],
            out_specs=pl.BlockSpec((MOE_SUB_ROWS, bn), lambda j, i, be, nu: (i, j)),
            scratch_shapes=[pltpu.VMEM((D_FF, bn), BF16)]),
        out_shape=jax.ShapeDtypeStruct((cap, D_MODEL), F32),
        compiler_params=_params(2, 56),
    )(block_e, n_used, a, wd)


def _combine_kernel(dest_ref, h_ref, r_ref, p_ref, wg_ref, wp_ref, g_ref, b_ref, ys_hbm, y_ref, buf, sem,
                    *, bm, row0, n_blocks):
    i = pl.program_id(0)

    def row_copy(src, slot, k, r):
        return pltpu.make_async_copy(ys_hbm.at[pl.ds(src, 1)], buf.at[slot, k, pl.ds(r, 1)], sem.at[slot])

    def issue(blk, slot):
        def body(r4, c):
            for q in range(DMA_ISSUE_UNROLL // TOP_K):
                r = r4 * (DMA_ISSUE_UNROLL // TOP_K) + q
                a = TOP_K * (row0 + blk * bm + r)
                for k in range(TOP_K):
                    row_copy(dest_ref[a + k], slot, k, r).start(priority=k % DMA_PRIORITIES)
            return c
        lax.fori_loop(0, bm // (DMA_ISSUE_UNROLL // TOP_K), body, 0)

    @pl.when(i == 0)
    def _():
        issue(0, 0)

    @pl.when(i + 1 < n_blocks)
    def _():
        issue(i + 1, (i + 1) % 2)

    slot = i % 2
    for k in range(TOP_K):
        pltpu.make_async_copy(ys_hbm.at[pl.ds(0, bm)], buf.at[slot, k], sem.at[slot]).wait()

    h = h_ref[...]
    r = r_ref[...]
    f = buf[slot, 0] * r[:, 2:3] + buf[slot, 1] * r[:, 3:4]
    gate = jax.nn.sigmoid(_dot(h.astype(BF16), wg_ref[...]))
    proj = _dot(p_ref[...].astype(BF16), wp_ref[...])
    y_ref[...] = _ln(ALPHA * h + f + gate * proj, g_ref[...], b_ref[...])


def _combine_ple_ln2(dest, h, route, p, ys, wg, wp, g, b, *, row0, n_rows):
    bm = 256
    n_blocks = n_rows // bm
    b0 = row0 // bm
    vec = pl.BlockSpec((1, D_MODEL), lambda i, d: (0, 0))
    return pl.pallas_call(
        functools.partial(_combine_kernel, bm=bm, row0=row0, n_blocks=n_blocks),
        grid_spec=pltpu.PrefetchScalarGridSpec(
            num_scalar_prefetch=1,
            grid=(n_blocks,),
            in_specs=[pl.BlockSpec((bm, D_MODEL), lambda i, d: (b0 + i, 0)),
                      pl.BlockSpec((bm, LANES), lambda i, d: (b0 + i, 0)),
                      pl.BlockSpec((bm, PLE_DIM), lambda i, d: (i, 0)),
                      pl.BlockSpec((D_MODEL, D_MODEL), lambda i, d: (0, 0)),
                      pl.BlockSpec((PLE_DIM, D_MODEL), lambda i, d: (0, 0)),
                      vec, vec,
                      pl.BlockSpec(memory_space=pl.ANY)],
            out_specs=pl.BlockSpec((bm, D_MODEL), lambda i, d: (i, 0)),
            scratch_shapes=[pltpu.VMEM((2, TOP_K, bm, D_MODEL), F32), pltpu.SemaphoreType.DMA((2,))]),
        out_shape=jax.ShapeDtypeStruct((n_rows, D_MODEL), F32),
        compiler_params=_params(1, 48),
    )(dest, h, route, p, wg, wp, g, b, ys)


def _route_plan(route):
    n = route.shape[0]
    n_assign = n * TOP_K
    flat_e = route[:, :TOP_K].astype(jnp.int32).reshape(n_assign)
    onehot = (flat_e[:, None] == jnp.arange(N_EXPERTS, dtype=jnp.int32)[None, :]).astype(jnp.int32)
    csum = jnp.cumsum(onehot, axis=0)
    counts = csum[-1]
    rank = jnp.sum(csum * onehot, axis=1) - 1
    padded = (counts + MOE_ROWS - 1) // MOE_ROWS * MOE_ROWS
    pad_end = jnp.cumsum(padded)
    pad_start = jnp.sum(onehot * (pad_end - padded)[None, :], axis=1)
    dest = (pad_start + rank).astype(jnp.int32)
    n_blocks = -(-n_assign // MOE_ROWS) + N_EXPERTS - 1
    block_start = jnp.arange(n_blocks, dtype=pad_end.dtype) * MOE_ROWS
    block_e = jnp.minimum(jnp.sum(pad_end[None, :] <= block_start[:, None], axis=1), N_EXPERTS - 1).astype(jnp.int32)
    n_used = (pad_end[-1:] // MOE_ROWS).astype(jnp.int32)
    sub = MOE_ROWS // MOE_SUB_ROWS
    sub_block_e = jnp.repeat(block_e, sub)
    sorted_tok = (jnp.argsort(flat_e, stable=True) // TOP_K).astype(jnp.int32)
    sorted_tok = jnp.concatenate([sorted_tok, sorted_tok[:MOE_SUB_ROWS]])
    offset = jnp.cumsum(counts) - counts - (pad_end - padded)
    sel = (sub_block_e[:, None] == jnp.arange(N_EXPERTS, dtype=jnp.int32)[None, :]).astype(jnp.int32)
    sub_start = jnp.arange(n_blocks * sub, dtype=jnp.int32) * MOE_SUB_ROWS
    block_base = (jnp.sum(sel * offset[None, :], axis=1) + sub_start) % n_assign
    return dest, sorted_tok, block_base.astype(jnp.int32), block_e, n_used, sub_block_e, n_used * sub


def kernel(x_prompt, x_sample, state_conv, cache_k, cache_v, p_prompt, p_sample, conv_w_pw1, conv_b_pw1, conv_w_dw, conv_b_dw, conv_ln_g, conv_ln_b, conv_w_pw2, conv_b_pw2, w_kv, rel_bias, attn_w_q, attn_sinks, attn_w_o, ln1_g, ln1_b, ln2_g, ln2_b, ffn_w_gate, ffn_w_up, ffn_w_down, moe_w_router, moe_b_router, moe_w_gate, moe_w_up, moe_w_down, ple_w_gate, ple_w_proj):
    n_pb, p_seq, _ = x_prompt.shape
    n_sb, s_seq, _ = x_sample.shape
    n_p, n_s = n_pb * p_seq, n_sb * s_seq
    vec = lambda v: v.reshape(1, -1).astype(F32)

    xp = x_prompt.reshape(n_p, D_MODEL)
    xs = x_sample.reshape(n_s, D_MODEL)
    w1 = conv_w_pw1[0].astype(BF16)
    b1 = vec(conv_b_pw1[0])
    u_p = _pw1_glu(xp, w1, b1)
    u_s = _pw1_glu(xs, w1, b1)
    w_dw, b_dw = conv_w_dw[0].astype(F32), vec(conv_b_dw[0])
    state = jnp.pad(state_conv[0], ((0, 0), (CONV_HALO - (CONV_WIDTH - 1), 0), (0, 0))).reshape(n_sb * CONV_HALO, D_MODEL)
    yc_p = _dwconv(u_p, u_p, w_dw, b_dw, n_seq=n_pb, seq=p_seq, bt=256, zero_first=True)
    yc_s = _dwconv(u_s, state, w_dw, b_dw, n_seq=n_sb, seq=s_seq, bt=s_seq, zero_first=False)
    w2 = conv_w_pw2[0].astype(BF16)
    tail0 = (w2, vec(conv_b_pw2[0]), vec(conv_ln_g[0]), vec(conv_ln_b[0]), vec(ln1_g[0]), vec(ln1_b[0]))
    h, hb = _pw2_ln1(yc_p, yc_s, xp, xs, *tail0)
    f = _ffn(hb, ffn_w_gate[0].astype(BF16), ffn_w_up[0].astype(BF16), ffn_w_down[0].astype(BF16))
    x1, x1b = _ple_ln2(h, hb, f, p_prompt[0].reshape(n_p, PLE_DIM), p_sample[0].reshape(n_s, PLE_DIM),
                       ple_w_gate[0].astype(BF16), ple_w_proj[0].astype(BF16), vec(ln2_g[0]), vec(ln2_b[0]))

    q = _mm(x1b, attn_w_q[0].astype(BF16), scale=HEAD_DIM ** -0.5, out_dtype=BF16)
    kv = _mm(x1b, w_kv.astype(BF16), scale=1.0, out_dtype=F32)
    bias = _rel_bias(rel_bias)
    sinks = jnp.repeat(attn_sinks[0].astype(F32), CHUNK).reshape(N_KV_HEADS, GROUP * CHUNK, 1)
    o_p = _attention(q, kv, kv, kv, bias, sinks, n_seq=n_pb, seq=p_seq, tq=256, row0=0, mask_first=True)
    o_s = _attention(q, kv, cache_k.reshape(n_sb * WINDOW, KV_DIM), cache_v.reshape(n_sb * WINDOW, KV_DIM),
                     bias, sinks, n_seq=n_sb, seq=s_seq, tq=s_seq, row0=n_p, mask_first=False)
    h1, h1p, route = _oproj_ln1_route(o_p, o_s, x1, attn_w_o[0].astype(BF16), vec(ln1_g[1]), vec(ln1_b[1]),
                                      moe_w_router[0], moe_b_router[0])

    dest, sorted_tok, block_base, block_e, n_used, sub_block_e, sub_n_used = _route_plan(route)
    xg = _gather_rows(sorted_tok, block_base, h1p)
    a = _moe_up(block_e, n_used, xg, moe_w_gate[0], moe_w_up[0])
    ys = _moe_down(sub_block_e, sub_n_used, a, moe_w_down[0])
    tail1 = (ple_w_gate[1].astype(BF16), ple_w_proj[1].astype(BF16), vec(ln2_g[1]), vec(ln2_b[1]))
    y_p = _combine_ple_ln2(dest, h1, route, p_prompt[1].reshape(n_p, PLE_DIM), ys, *tail1, row0=0, n_rows=n_p)
    y_s = _combine_ple_ln2(dest, h1, route, p_sample[1].reshape(n_s, PLE_DIM), ys, *tail1, row0=n_p, n_rows=n_s)

    keep = CONV_WIDTH - 1
    conv_p = u_p.reshape(n_pb, p_seq, D_MODEL)[None, :, p_seq - keep:]
    conv_s = u_s.reshape(n_sb, s_seq, D_MODEL)[None, :, s_seq - keep:]
    wps = p_seq // WINDOW
    kv_p = kv.reshape(-1, WINDOW, 2 * KV_DIM)[wps - 1:n_pb * wps:wps].reshape(n_pb, WINDOW, 2, N_KV_HEADS, HEAD_DIM)
    kv_s = kv[n_p:].reshape(n_sb, s_seq, 2, N_KV_HEADS, HEAD_DIM)
    k_s = jnp.concatenate([cache_k, kv_s[:, :, 0]], axis=1)[:, s_seq:]
    v_s = jnp.concatenate([cache_v, kv_s[:, :, 1]], axis=1)[:, s_seq:]
    return (y_p.reshape(n_pb, p_seq, D_MODEL), y_s.reshape(n_sb, s_seq, D_MODEL), conv_p, conv_s,
            kv_p[:, :, 0], kv_p[:, :, 1], k_s, v_s)
```

```python
import functools
import math

import jax
import jax.numpy as jnp
from jax import lax
from jax.experimental import pallas as pl
from jax.experimental.pallas import tpu as pltpu

D_MODEL = 2048
CHUNK = 64
CONV_WIDTH = 31
HEAD_DIM = 64
N_HEADS = 32
N_KV_HEADS = 8
GROUP = N_HEADS // N_KV_HEADS
WINDOW = 128
BAND = WINDOW + CHUNK
NUM_BUCKETS = 32
MAX_DISTANCE = 128
D_FF = 5632
N_EXPERTS = 8
TOP_K = 2
PLE_DIM = 256
DEPTH = 2
ALPHA = (2 * DEPTH) ** 0.25
LN_EPS = 1e-5
NEG_INF = -1e30

KV_DIM = N_KV_HEADS * HEAD_DIM
CONV_HALO = 32
MOE_ROWS = 1024
MOE_SUB_ROWS = 512
LANES = 128
SUBLANES = 8
PACKED_TILES = D_MODEL // 2 // LANES
DMA_ISSUE_UNROLL = 8
DMA_PRIORITIES = 2
MIB = 1024 * 1024

F32 = jnp.float32
BF16 = jnp.bfloat16


def _params(n_axes, vmem_mib):
    return pltpu.CompilerParams(dimension_semantics=("arbitrary",) * n_axes,
                                vmem_limit_bytes=vmem_mib * MIB)


def _dot(a, b):
    return jnp.dot(a, b, preferred_element_type=F32)


def _ln(x, g, b):
    mu = jnp.mean(x, axis=-1, keepdims=True)
    xc = x - mu
    var = jnp.mean(xc * xc, axis=-1, keepdims=True)
    return xc * lax.rsqrt(var + LN_EPS) * g + b


def _silu(x):
    return x * jax.nn.sigmoid(x)


def _pw1_glu_kernel(x_ref, wa_ref, wb_ref, ba_ref, bb_ref, u_ref, xb_ref):
    @pl.when(pl.program_id(1) == 0)
    def _():
        xb_ref[...] = x_ref[...].astype(BF16)

    xb = xb_ref[...]
    a = _dot(xb, wa_ref[...]) + ba_ref[...]
    g = _dot(xb, wb_ref[...]) + bb_ref[...]
    u_ref[...] = a * jax.nn.sigmoid(g)


def _pw1_glu(x, w, b):
    n = x.shape[0]
    bm, bn = math.gcd(n, 1024), 512
    nj = D_MODEL // bn
    return pl.pallas_call(
        _pw1_glu_kernel,
        grid=(n // bm, nj),
        in_specs=[pl.BlockSpec((bm, D_MODEL), lambda i, j: (i, 0)),
                  pl.BlockSpec((D_MODEL, bn), lambda i, j: (0, j)),
                  pl.BlockSpec((D_MODEL, bn), lambda i, j: (0, j + nj)),
                  pl.BlockSpec((1, bn), lambda i, j: (0, j)),
                  pl.BlockSpec((1, bn), lambda i, j: (0, j + nj))],
        out_specs=pl.BlockSpec((bm, bn), lambda i, j: (i, j)),
        out_shape=jax.ShapeDtypeStruct((n, D_MODEL), F32),
        scratch_shapes=[pltpu.VMEM((bm, D_MODEL), BF16)],
        compiler_params=_params(2, 48),
    )(x, w, w, b, b)


def _dwconv_kernel(prev_ref, cur_ref, w_ref, b_ref, y_ref, e_ref, *, bt, zero_first, rc):
    prev = prev_ref[...]
    if zero_first:
        prev = jnp.where(pl.program_id(1) == 0, 0.0, prev)
    e_ref[0, 0:CONV_HALO, :] = prev
    e_ref[0, CONV_HALO:, :] = cur_ref[...]
    e0 = e_ref[0]
    n_e = bt + CONV_HALO
    for rho in range(1, SUBLANES):
        e_ref[rho] = pltpu.roll(e0, n_e - rho, axis=0)
    off = CONV_HALO - (CONV_WIDTH - 1)
    for r0 in range(0, bt, rc):
        acc = jnp.broadcast_to(b_ref[...], (rc, b_ref.shape[1]))
        for k in range(CONV_WIDTH):
            rho = (k + off) % SUBLANES
            base = r0 + k + off - rho
            acc = acc + e_ref[rho, base:base + rc, :] * w_ref[k:k + 1, :]
        y_ref[r0:r0 + rc, :] = acc


def _dwconv(u, prev_src, w, b, *, n_seq, seq, bt, zero_first):
    bc = 512
    nt = seq // bt
    hb = bt // CONV_HALO
    if zero_first:
        prev_map = lambda s, t, c: (jnp.maximum((s * nt + t) * hb - 1, 0), c)
    else:
        prev_map = lambda s, t, c: (s, c)
    return pl.pallas_call(
        functools.partial(_dwconv_kernel, bt=bt, zero_first=zero_first, rc=32),
        grid=(n_seq, nt, D_MODEL // bc),
        in_specs=[pl.BlockSpec((CONV_HALO, bc), prev_map),
                  pl.BlockSpec((bt, bc), lambda s, t, c: (s * nt + t, c)),
                  pl.BlockSpec((CONV_WIDTH, bc), lambda s, t, c: (0, c)),
                  pl.BlockSpec((1, bc), lambda s, t, c: (0, c))],
        out_specs=pl.BlockSpec((bt, bc), lambda s, t, c: (s * nt + t, c)),
        out_shape=jax.ShapeDtypeStruct(u.shape, F32),
        scratch_shapes=[pltpu.VMEM((SUBLANES, bt + CONV_HALO, bc), F32)],
        compiler_params=_params(3, 32),
    )(prev_src, u, w, b)


def _two_group_specs(bm, width, n_first):
    nb = n_first // bm
    return (pl.BlockSpec((bm, width), lambda i, *_: (jnp.minimum(i, nb - 1), 0)),
            pl.BlockSpec((bm, width), lambda i, *_: (jnp.maximum(i - nb, 0), 0)))


def _pick_group(first_ref, second_ref, n_first_blocks):
    return jnp.where(pl.program_id(0) < n_first_blocks, first_ref[...], second_ref[...])


def _pw2_ln1_kernel(yp_ref, ys_ref, xp_ref, xs_ref, w_ref, b2_ref, cg_ref, cb_ref, g_ref, b_ref, h_ref, hb_ref,
                    *, nb_first):
    y = _silu(_ln(_pick_group(yp_ref, ys_ref, nb_first), cg_ref[...], cb_ref[...]))
    mix = _dot(y.astype(BF16), w_ref[...]) + b2_ref[...]
    h = _ln(ALPHA * _pick_group(xp_ref, xs_ref, nb_first) + mix, g_ref[...], b_ref[...])
    h_ref[...] = h
    hb_ref[...] = h.astype(BF16)


def _pw2_ln1(y_p, y_s, x_p, x_s, w, b2, cg, cb, g, b):
    n_p, n = y_p.shape[0], y_p.shape[0] + y_s.shape[0]
    bm = 256
    row = pl.BlockSpec((bm, D_MODEL), lambda i: (i, 0))
    vec = pl.BlockSpec((1, D_MODEL), lambda i: (0, 0))
    return pl.pallas_call(
        functools.partial(_pw2_ln1_kernel, nb_first=n_p // bm),
        grid=(n // bm,),
        in_specs=[*_two_group_specs(bm, D_MODEL, n_p), *_two_group_specs(bm, D_MODEL, n_p),
                  pl.BlockSpec((D_MODEL, D_MODEL), lambda i: (0, 0)), vec, vec, vec, vec, vec],
        out_specs=[row, row],
        out_shape=[jax.ShapeDtypeStruct((n, D_MODEL), F32), jax.ShapeDtypeStruct((n, D_MODEL), BF16)],
        compiler_params=_params(1, 56),
    )(y_p, y_s, x_p, x_s, w, b2, cg, cb, g, b)


def _ffn_kernel(x_ref, wg_ref, wu_ref, wd_ref, o_ref):
    j = pl.program_id(1)
    x = x_ref[...]
    g = _dot(x, wg_ref[...])
    u = _dot(x, wu_ref[...])
    part = _dot((_silu(g) * u).astype(BF16), wd_ref[...])

    @pl.when(j == 0)
    def _():
        o_ref[...] = part

    @pl.when(j > 0)
    def _():
        o_ref[...] += part


def _ffn(xb, wg, wu, wd):
    n = xb.shape[0]
    bm, bf = math.gcd(n, 1024), 512
    return pl.pallas_call(
        _ffn_kernel,
        grid=(n // bm, D_FF // bf),
        in_specs=[pl.BlockSpec((bm, D_MODEL), lambda i, j: (i, 0)),
                  pl.BlockSpec((D_MODEL, bf), lambda i, j: (0, j)),
                  pl.BlockSpec((D_MODEL, bf), lambda i, j: (0, j)),
                  pl.BlockSpec((bf, D_MODEL), lambda i, j: (j, 0))],
        out_specs=pl.BlockSpec((bm, D_MODEL), lambda i, j: (i, 0)),
        out_shape=jax.ShapeDtypeStruct((n, D_MODEL), F32),
        compiler_params=_params(2, 60),
    )(xb, wg, wu, wd)


def _ple_ln2_kernel(h_ref, hb_ref, f_ref, pp_ref, ps_ref, wg_ref, wp_ref, g_ref, b_ref, x_ref, xb_ref, *, nb_first):
    gate = jax.nn.sigmoid(_dot(hb_ref[...], wg_ref[...]))
    proj = _dot(_pick_group(pp_ref, ps_ref, nb_first).astype(BF16), wp_ref[...])
    x = _ln(ALPHA * h_ref[...] + f_ref[...] + gate * proj, g_ref[...], b_ref[...])
    x_ref[...] = x
    xb_ref[...] = x.astype(BF16)


def _ple_ln2(h, hb, f, p_p, p_s, wg, wp, g, b):
    n = h.shape[0]
    bm = 256
    row = pl.BlockSpec((bm, D_MODEL), lambda i: (i, 0))
    vec = pl.BlockSpec((1, D_MODEL), lambda i: (0, 0))
    return pl.pallas_call(
        functools.partial(_ple_ln2_kernel, nb_first=p_p.shape[0] // bm),
        grid=(n // bm,),
        in_specs=[row, row, row, *_two_group_specs(bm, PLE_DIM, p_p.shape[0]),
                  pl.BlockSpec((D_MODEL, D_MODEL), lambda i: (0, 0)),
                  pl.BlockSpec((PLE_DIM, D_MODEL), lambda i: (0, 0)), vec, vec],
        out_specs=[row, row],
        out_shape=[jax.ShapeDtypeStruct((n, D_MODEL), F32), jax.ShapeDtypeStruct((n, D_MODEL), BF16)],
        compiler_params=_params(1, 48),
    )(h, hb, f, p_p, p_s, wg, wp, g, b)


def _mm_kernel(x_ref, w_ref, o_ref, *, scale):
    acc = _dot(x_ref[...], w_ref[...])
    if scale != 1.0:
        acc = acc * scale
    o_ref[...] = acc.astype(o_ref.dtype)


def _mm(xb, w, *, scale, out_dtype):
    n, k = xb.shape
    m = w.shape[1]
    bm, bn = math.gcd(n, 1024), 512
    return pl.pallas_call(
        functools.partial(_mm_kernel, scale=scale),
        grid=(n // bm, m // bn),
        in_specs=[pl.BlockSpec((bm, k), lambda i, j: (i, 0)),
                  pl.BlockSpec((k, bn), lambda i, j: (0, j))],
        out_specs=pl.BlockSpec((bm, bn), lambda i, j: (i, j)),
        out_shape=jax.ShapeDtypeStruct((n, m), out_dtype),
        compiler_params=_params(2, 32),
    )(xb, w)


def _t5_bucket(rel):
    half = NUM_BUCKETS // 2
    exact = half // 2
    ret = jnp.where(rel > 0, half, 0)
    n = jnp.abs(rel)
    large = exact + (jnp.log(jnp.maximum(n, 1).astype(F32) / exact)
                     / math.log(MAX_DISTANCE / exact) * (half - exact)).astype(jnp.int32)
    large = jnp.minimum(large, half - 1)
    return ret + jnp.where(n < exact, n, large)


def _bias_kernel(table_ref, bucket_ref, o_ref):
    bk = bucket_ref[...]
    masks = [bk == b for b in range(NUM_BUCKETS)]
    for h in range(N_HEADS):
        acc = jnp.zeros(bk.shape, F32)
        for b in range(NUM_BUCKETS):
            acc = jnp.where(masks[b], table_ref[b, h], acc)
        o_ref[h] = acc


def _rel_bias(table):
    i = jnp.arange(CHUNK)[:, None]
    j = jnp.arange(BAND)[None, :]
    bucket = _t5_bucket((j - WINDOW) - i).astype(jnp.int32)
    out = pl.pallas_call(
        _bias_kernel,
        in_specs=[pl.BlockSpec(memory_space=pltpu.SMEM), pl.BlockSpec(memory_space=pltpu.VMEM)],
        out_specs=pl.BlockSpec(memory_space=pltpu.VMEM),
        out_shape=jax.ShapeDtypeStruct((N_HEADS, CHUNK, BAND), F32),
    )(table.astype(F32), bucket)
    return out.reshape(N_KV_HEADS, GROUP * CHUNK, BAND)


def _attn_kernel(q_ref, kp_ref, vp_ref, kc_ref, vc_ref, bias_ref, sink_ref, o_ref, q_scr, k_scr, v_scr,
                 *, tq, mask_first):
    t = pl.program_id(1)
    for h in range(N_KV_HEADS):
        hs = slice(h * HEAD_DIM, (h + 1) * HEAD_DIM)
        k_scr[h, 0:WINDOW, :] = kp_ref[:, hs].astype(BF16)
        k_scr[h, WINDOW:, :] = kc_ref[:, hs].astype(BF16)
        v_scr[h, 0:WINDOW, :] = vp_ref[:, hs].astype(BF16)
        v_scr[h, WINDOW:, :] = vc_ref[:, hs].astype(BF16)
    for h in range(N_HEADS):
        q_scr[h] = q_ref[:, h * HEAD_DIM:(h + 1) * HEAD_DIM]
    rows = GROUP * CHUNK
    bias = bias_ref[...]
    sink = sink_ref[...]
    n_chunks = tq // CHUNK
    scores = []
    for c in range(n_chunks):
        cs = slice(c * CHUNK, (c + 1) * CHUNK)
        ks = slice(c * CHUNK, c * CHUNK + BAND)
        qh = q_scr[:, cs, :].reshape(N_KV_HEADS, rows, HEAD_DIM)
        scores.append(jnp.einsum('hqd,hkd->hqk', qh, k_scr[:, ks, :], preferred_element_type=F32))
    probs = []
    for c in range(n_chunks):
        s = scores[c] + bias
        if mask_first and c * CHUNK < WINDOW:
            first_valid = WINDOW - (t * tq + c * CHUNK)
            kidx = lax.broadcasted_iota(jnp.int32, s.shape, 2)
            s = jnp.where(kidx >= first_valid, s, NEG_INF)
        m = jnp.maximum(jnp.max(s, axis=-1, keepdims=True), sink)
        e = jnp.exp(s - m)
        den = jnp.sum(e, axis=-1, keepdims=True) + jnp.exp(sink - m)
        probs.append((e * (1.0 / den)).astype(BF16))
    for c in range(n_chunks):
        cs = slice(c * CHUNK, (c + 1) * CHUNK)
        ks = slice(c * CHUNK, c * CHUNK + BAND)
        o = jnp.einsum('hqk,hkd->hqd', probs[c], v_scr[:, ks, :], preferred_element_type=F32)
        o = o.reshape(N_HEADS, CHUNK, HEAD_DIM).astype(o_ref.dtype)
        for h in range(N_HEADS):
            o_ref[cs, h * HEAD_DIM:(h + 1) * HEAD_DIM] = o[h]


def _attention(q, kv, k_prev, v_prev, bias, sinks, *, n_seq, seq, tq, row0, mask_first):
    nt = seq // tq
    qb0 = row0 // tq
    if mask_first:
        wpt = tq // WINDOW
        kp_spec = pl.BlockSpec((WINDOW, KV_DIM), lambda s, t: (jnp.maximum((s * nt + t) * wpt - 1, 0), 0))
        vp_spec = pl.BlockSpec((WINDOW, KV_DIM), lambda s, t: (jnp.maximum((s * nt + t) * wpt - 1, 0), 1))
    else:
        kp_spec = pl.BlockSpec((WINDOW, KV_DIM), lambda s, t: (s, 0))
        vp_spec = pl.BlockSpec((WINDOW, KV_DIM), lambda s, t: (s, 0))
    return pl.pallas_call(
        functools.partial(_attn_kernel, tq=tq, mask_first=mask_first),
        grid=(n_seq, nt),
        in_specs=[pl.BlockSpec((tq, D_MODEL), lambda s, t: (qb0 + s * nt + t, 0)),
                  kp_spec, vp_spec,
                  pl.BlockSpec((tq, KV_DIM), lambda s, t: (qb0 + s * nt + t, 0)),
                  pl.BlockSpec((tq, KV_DIM), lambda s, t: (qb0 + s * nt + t, 1)),
                  pl.BlockSpec((N_KV_HEADS, GROUP * CHUNK, BAND), lambda s, t: (0, 0, 0)),
                  pl.BlockSpec((N_KV_HEADS, GROUP * CHUNK, 1), lambda s, t: (0, 0, 0))],
        out_specs=pl.BlockSpec((tq, D_MODEL), lambda s, t: (s * nt + t, 0)),
        out_shape=jax.ShapeDtypeStruct((n_seq * seq, D_MODEL), BF16),
        scratch_shapes=[pltpu.VMEM((N_HEADS, tq, HEAD_DIM), BF16),
                        pltpu.VMEM((N_KV_HEADS, WINDOW + tq, HEAD_DIM), BF16),
                        pltpu.VMEM((N_KV_HEADS, WINDOW + tq, HEAD_DIM), BF16)],
        compiler_params=_params(2, 40),
    )(q, k_prev, v_prev, kv, kv, bias, sinks)


def _f32_bits(x):
    return lax.bitcast_convert_type(x, jnp.uint32)


def _pack_bf16_pairs(x):
    c = x.shape[1] // 2
    lo = _f32_bits(x[:, :c].astype(BF16).astype(F32))
    hi = _f32_bits(x[:, c:].astype(BF16).astype(F32))
    return hi | (lo >> 16)


def _unpack_bf16_pairs(w):
    lo = lax.bitcast_convert_type(w << 16, F32).astype(BF16)
    hi = lax.bitcast_convert_type(w & jnp.uint32(0xFFFF0000), F32).astype(BF16)
    return lo, hi


def _route(h, wh, wl, b):
    hh = h.astype(BF16)
    hl = (h - hh.astype(F32)).astype(BF16)
    logits = _dot(hh, wh) + (_dot(hl, wh) + _dot(hh, wl)) + b
    lane = lax.broadcasted_iota(jnp.int32, logits.shape, 1).astype(F32)
    ninf = -jnp.inf
    lg = jnp.where(lane < N_EXPERTS, logits, ninf)
    m1 = jnp.max(lg, axis=-1, keepdims=True)
    i1 = jnp.min(jnp.where(lg == m1, lane, float(LANES)), axis=-1, keepdims=True)
    lg2 = jnp.where(lane == i1, ninf, lg)
    m2 = jnp.max(lg2, axis=-1, keepdims=True)
    i2 = jnp.min(jnp.where(lg2 == m2, lane, float(LANES)), axis=-1, keepdims=True)
    ex = jnp.exp(m2 - m1)
    den = 1.0 + ex
    out = jnp.where(lane == 0, i1, jnp.where(lane == 1, i2, jnp.where(lane == 2, 1.0 / den, ex / den)))
    return jnp.where(lane < 4, out, 0.0)


def _oproj_ln1_kernel(op_ref, os_ref, x_ref, w_ref, g_ref, b_ref, rwh_ref, rwl_ref, rb_ref, h_ref, hp_ref, r_ref,
                      *, nb_first):
    mix = _dot(_pick_group(op_ref, os_ref, nb_first), w_ref[...])
    h = _ln(ALPHA * x_ref[...] + mix, g_ref[...], b_ref[...])
    h_ref[...] = h
    r_ref[...] = _route(h, rwh_ref[...], rwl_ref[...], rb_ref[...])
    packed = _pack_bf16_pairs(h)
    for j in range(PACKED_TILES):
        hp_ref[pl.ds(j, h.shape[0], stride=PACKED_TILES), :] = packed[:, j * LANES:(j + 1) * LANES]


def _oproj_ln1_route(o_p, o_s, x, w, g, b, w_router, b_router):
    n = x.shape[0]
    bm = 256
    row = pl.BlockSpec((bm, D_MODEL), lambda i: (i, 0))
    vec = pl.BlockSpec((1, D_MODEL), lambda i: (0, 0))
    rw = pl.BlockSpec((D_MODEL, LANES), lambda i: (0, 0))
    wp = jnp.zeros((D_MODEL, LANES), F32).at[:, :N_EXPERTS].set(w_router.astype(F32))
    wh = wp.astype(BF16)
    wl = (wp - wh.astype(F32)).astype(BF16)
    bp = jnp.zeros((1, LANES), F32).at[0, :N_EXPERTS].set(b_router.astype(F32))
    return pl.pallas_call(
        functools.partial(_oproj_ln1_kernel, nb_first=o_p.shape[0] // bm),
        grid=(n // bm,),
        in_specs=[*_two_group_specs(bm, D_MODEL, o_p.shape[0]), row,
                  pl.BlockSpec((D_MODEL, D_MODEL), lambda i: (0, 0)), vec, vec,
                  rw, rw, pl.BlockSpec((1, LANES), lambda i: (0, 0))],
        out_specs=[row, pl.BlockSpec((bm * PACKED_TILES, LANES), lambda i: (i, 0)),
                   pl.BlockSpec((bm, LANES), lambda i: (i, 0))],
        out_shape=[jax.ShapeDtypeStruct((n, D_MODEL), F32),
                   jax.ShapeDtypeStruct((n * PACKED_TILES, LANES), jnp.uint32),
                   jax.ShapeDtypeStruct((n, LANES), F32)],
        compiler_params=_params(1, 48),
    )(o_p, o_s, x, w, g, b, wh, wl, bp)


def _gather_kernel(tok_ref, base_ref, h_hbm, o_ref, buf, sem, *, rows, n_blocks):
    i = pl.program_id(0)

    def row_copy(tok, slot, r):
        src = pl.multiple_of(tok * PACKED_TILES, PACKED_TILES)
        dst = pl.multiple_of(r * PACKED_TILES, PACKED_TILES)
        return pltpu.make_async_copy(h_hbm.at[pl.ds(src, PACKED_TILES)], buf.at[slot, pl.ds(dst, PACKED_TILES)],
                                     sem.at[slot])

    def issue(blk, slot):
        base = base_ref[blk]

        def body(r8, c):
            for q in range(DMA_ISSUE_UNROLL):
                r = r8 * DMA_ISSUE_UNROLL + q
                row_copy(tok_ref[base + r], slot, r).start(priority=q % DMA_PRIORITIES)
            return c
        lax.fori_loop(0, rows // DMA_ISSUE_UNROLL, body, 0)

    @pl.when(i == 0)
    def _():
        issue(0, 0)

    @pl.when(i + 1 < n_blocks)
    def _():
        issue(i + 1, (i + 1) % 2)

    slot = i % 2
    pltpu.make_async_copy(h_hbm.at[pl.ds(0, rows * PACKED_TILES)], buf.at[slot], sem.at[slot]).wait()
    half = D_MODEL // 2
    for j in range(PACKED_TILES):
        lo, hi = _unpack_bf16_pairs(buf[slot, pl.ds(j, rows, stride=PACKED_TILES), :])
        o_ref[:, j * LANES:(j + 1) * LANES] = lo
        o_ref[:, half + j * LANES:half + (j + 1) * LANES] = hi


def _gather_rows(sorted_tok, block_base, hp):
    rows = MOE_SUB_ROWS
    n_blocks = block_base.shape[0]
    return pl.pallas_call(
        functools.partial(_gather_kernel, rows=rows, n_blocks=n_blocks),
        grid_spec=pltpu.PrefetchScalarGridSpec(
            num_scalar_prefetch=2,
            grid=(n_blocks,),
            in_specs=[pl.BlockSpec(memory_space=pl.ANY)],
            out_specs=pl.BlockSpec((rows, D_MODEL), lambda i, tok, base: (i, 0)),
            scratch_shapes=[pltpu.VMEM((2, rows * PACKED_TILES, LANES), jnp.uint32),
                            pltpu.SemaphoreType.DMA((2,))]),
        out_shape=jax.ShapeDtypeStruct((n_blocks * rows, D_MODEL), BF16),
        compiler_params=_params(1, 32),
    )(sorted_tok, block_base, hp)


def _expert_changed(be_ref, i):
    return jnp.logical_or(i == 0, be_ref[i] != be_ref[jnp.maximum(i - 1, 0)])


def _moe_up_kernel(be_ref, sv_ref, ws_ref, x_ref, wg_ref, wu_ref, a_ref, wgb_ref, wub_ref, *, n_blocks):
    i = pl.program_id(1)
    sub = MOE_ROWS // MOE_SUB_ROWS
    nxt = jnp.minimum(i, n_blocks - 1)
    stage = jnp.logical_and(jnp.logical_and(i < n_blocks, sv_ref[sub * nxt] > 0), _expert_changed(be_ref, nxt))

    @pl.when(stage)
    def _():
        s = ws_ref[nxt]
        wgb_ref[s] = wg_ref[...].astype(BF16)
        wub_ref[s] = wu_ref[...].astype(BF16)

    b = jnp.maximum(i - 1, 0)
    live = i > 0
    used = jnp.logical_and(live, sv_ref[sub * b] > 0)
    full = jnp.logical_and(live, sv_ref[sub * b + sub - 1] > 0)

    def swiglu(x):
        s = ws_ref[b]
        g = _dot(x, wgb_ref[s])
        u = _dot(x, wub_ref[s])
        return (_silu(g) * u).astype(a_ref.dtype)

    @pl.when(full)
    def _():
        a_ref[...] = swiglu(x_ref[...])

    @pl.when(jnp.logical_and(used, jnp.logical_not(full)))
    def _():
        a_ref[0:MOE_SUB_ROWS, :] = swiglu(x_ref[0:MOE_SUB_ROWS, :])
        a_ref[MOE_SUB_ROWS:, :] = jnp.zeros((MOE_ROWS - MOE_SUB_ROWS, a_ref.shape[1]), a_ref.dtype)

    @pl.when(jnp.logical_and(live, jnp.logical_not(used)))
    def _():
        a_ref[...] = jnp.zeros(a_ref.shape, a_ref.dtype)


def _moe_up(block_e, sub_valid, xs, wg, wu):
    assert MOE_ROWS == 2 * MOE_SUB_ROWS
    cap = xs.shape[0]
    bf = 512
    nb = cap // MOE_ROWS
    changes = jnp.concatenate([jnp.zeros((1,), jnp.int32), (block_e[1:] != block_e[:-1]).astype(jnp.int32)])
    w_slot = (jnp.cumsum(changes) % 2).astype(jnp.int32)
    row = lambda j, i, be, sv, ws: (jnp.maximum(i - 1, 0), 0)
    wmap = lambda j, i, be, sv, ws: (be[jnp.minimum(i, nb - 1)], 0, j)
    return pl.pallas_call(
        functools.partial(_moe_up_kernel, n_blocks=nb),
        grid_spec=pltpu.PrefetchScalarGridSpec(
            num_scalar_prefetch=3,
            grid=(D_FF // bf, nb + 1),
            in_specs=[pl.BlockSpec((MOE_ROWS, D_MODEL), row),
                      pl.BlockSpec((None, D_MODEL, bf), wmap),
                      pl.BlockSpec((None, D_MODEL, bf), wmap)],
            out_specs=pl.BlockSpec((MOE_ROWS, bf), lambda j, i, be, sv, ws: (jnp.maximum(i - 1, 0), j)),
            scratch_shapes=[pltpu.VMEM((2, D_MODEL, bf), BF16), pltpu.VMEM((2, D_MODEL, bf), BF16)]),
        out_shape=jax.ShapeDtypeStruct((cap, D_FF), BF16),
        compiler_params=_params(2, 56),
    )(block_e, sub_valid, w_slot, xs, wg, wu)


def _moe_down_kernel(be_ref, sv_ref, a_ref, wd_ref, y_ref, wdb_ref):
    i = pl.program_id(1)
    used = sv_ref[i] > 0

    @pl.when(jnp.logical_and(used, _expert_changed(be_ref, i)))
    def _():
        wdb_ref[...] = wd_ref[...].astype(BF16)

    @pl.when(used)
    def _():
        y_ref[...] = _dot(a_ref[...], wdb_ref[...])

    @pl.when(jnp.logical_not(used))
    def _():
        y_ref[...] = jnp.zeros(y_ref.shape, y_ref.dtype)


def _moe_down(block_e, sub_valid, a, wd):
    cap = a.shape[0]
    bn = 512
    return pl.pallas_call(
        _moe_down_kernel,
        grid_spec=pltpu.PrefetchScalarGridSpec(
            num_scalar_prefetch=2,
            grid=(D_MODEL // bn, cap // MOE_SUB_ROWS),
            in_specs=[pl.BlockSpec((MOE_SUB_ROWS, D_FF), lambda j, i, be, nu: (i, 0)),
                      pl.BlockSpec((None, D_FF, bn), lambda j, i, be, nu: (be[i], 0, j))],
            out_specs=pl.BlockSpec((MOE_SUB_ROWS, bn), lambda j, i, be, nu: (i, j)),
            scratch_shapes=[pltpu.VMEM((D_FF, bn), BF16)]),
        out_shape=jax.ShapeDtypeStruct((cap, D_MODEL), F32),
        compiler_params=_params(2, 56),
    )(block_e, sub_valid, a, wd)


def _combine_kernel(dest_ref, h_ref, r_ref, p_ref, wg_ref, wp_ref, g_ref, b_ref, ys_hbm, y_ref, buf, sem,
                    *, bm, row0, n_blocks):
    i = pl.program_id(0)

    def row_copy(src, slot, k, r):
        return pltpu.make_async_copy(ys_hbm.at[pl.ds(src, 1)], buf.at[slot, k, pl.ds(r, 1)], sem.at[slot])

    def issue(blk, slot):
        def body(r4, c):
            for q in range(DMA_ISSUE_UNROLL // TOP_K):
                r = r4 * (DMA_ISSUE_UNROLL // TOP_K) + q
                a = TOP_K * (row0 + blk * bm + r)
                for k in range(TOP_K):
                    row_copy(dest_ref[a + k], slot, k, r).start(priority=k % DMA_PRIORITIES)
            return c
        lax.fori_loop(0, bm // (DMA_ISSUE_UNROLL // TOP_K), body, 0)

    @pl.when(i == 0)
    def _():
        issue(0, 0)

    @pl.when(i + 1 < n_blocks)
    def _():
        issue(i + 1, (i + 1) % 2)

    slot = i % 2
    for k in range(TOP_K):
        pltpu.make_async_copy(ys_hbm.at[pl.ds(0, bm)], buf.at[slot, k], sem.at[slot]).wait()

    h = h_ref[...]
    r = r_ref[...]
    f = buf[slot, 0] * r[:, 2:3] + buf[slot, 1] * r[:, 3:4]
    gate = jax.nn.sigmoid(_dot(h.astype(BF16), wg_ref[...]))
    proj = _dot(p_ref[...].astype(BF16), wp_ref[...])
    y_ref[...] = _ln(ALPHA * h + f + gate * proj, g_ref[...], b_ref[...])


def _combine_ple_ln2(dest, h, route, p, ys, wg, wp, g, b, *, row0, n_rows):
    bm = 256
    n_blocks = n_rows // bm
    b0 = row0 // bm
    vec = pl.BlockSpec((1, D_MODEL), lambda i, d: (0, 0))
    return pl.pallas_call(
        functools.partial(_combine_kernel, bm=bm, row0=row0, n_blocks=n_blocks),
        grid_spec=pltpu.PrefetchScalarGridSpec(
            num_scalar_prefetch=1,
            grid=(n_blocks,),
            in_specs=[pl.BlockSpec((bm, D_MODEL), lambda i, d: (b0 + i, 0)),
                      pl.BlockSpec((bm, LANES), lambda i, d: (b0 + i, 0)),
                      pl.BlockSpec((bm, PLE_DIM), lambda i, d: (i, 0)),
                      pl.BlockSpec((D_MODEL, D_MODEL), lambda i, d: (0, 0)),
                      pl.BlockSpec((PLE_DIM, D_MODEL), lambda i, d: (0, 0)),
                      vec, vec,
                      pl.BlockSpec(memory_space=pl.ANY)],
            out_specs=pl.BlockSpec((bm, D_MODEL), lambda i, d: (i, 0)),
            scratch_shapes=[pltpu.VMEM((2, TOP_K, bm, D_MODEL), F32), pltpu.SemaphoreType.DMA((2,))]),
        out_shape=jax.ShapeDtypeStruct((n_rows, D_MODEL), F32),
        compiler_params=_params(1, 48),
    )(dest, h, route, p, wg, wp, g, b, ys)


def _route_plan(route):
    n = route.shape[0]
    n_assign = n * TOP_K
    flat_e = route[:, :TOP_K].astype(jnp.int32).reshape(n_assign)
    onehot = (flat_e[:, None] == jnp.arange(N_EXPERTS, dtype=jnp.int32)[None, :]).astype(jnp.int32)
    csum = jnp.cumsum(onehot, axis=0)
    counts = csum[-1]
    rank = jnp.sum(csum * onehot, axis=1) - 1
    padded = (counts + MOE_ROWS - 1) // MOE_ROWS * MOE_ROWS
    pad_end = jnp.cumsum(padded)
    pad_start = jnp.sum(onehot * (pad_end - padded)[None, :], axis=1)
    dest = (pad_start + rank).astype(jnp.int32)
    n_blocks = -(-n_assign // MOE_ROWS) + N_EXPERTS - 1
    block_start = jnp.arange(n_blocks, dtype=pad_end.dtype) * MOE_ROWS
    block_e = jnp.minimum(jnp.sum(pad_end[None, :] <= block_start[:, None], axis=1), N_EXPERTS - 1).astype(jnp.int32)
    sub = MOE_ROWS // MOE_SUB_ROWS
    sub_block_e = jnp.repeat(block_e, sub)
    sorted_tok = (jnp.argsort(flat_e, stable=True) // TOP_K).astype(jnp.int32)
    sorted_tok = jnp.concatenate([sorted_tok, sorted_tok[:MOE_SUB_ROWS]])
    offset = jnp.cumsum(counts) - counts - (pad_end - padded)
    sel = (sub_block_e[:, None] == jnp.arange(N_EXPERTS, dtype=jnp.int32)[None, :]).astype(jnp.int32)
    sub_start = jnp.arange(n_blocks * sub, dtype=jnp.int32) * MOE_SUB_ROWS
    block_base = (jnp.sum(sel * offset[None, :], axis=1) + sub_start) % n_assign
    real_end = pad_end - padded + counts
    sub_valid = (sub_start < jnp.sum(sel * real_end[None, :], axis=1)).astype(jnp.int32)
    return dest, sorted_tok, block_base.astype(jnp.int32), block_e, sub_block_e, sub_valid


def kernel(x_prompt, x_sample, state_conv, cache_k, cache_v, p_prompt, p_sample, conv_w_pw1, conv_b_pw1, conv_w_dw, conv_b_dw, conv_ln_g, conv_ln_b, conv_w_pw2, conv_b_pw2, w_kv, rel_bias, attn_w_q, attn_sinks, attn_w_o, ln1_g, ln1_b, ln2_g, ln2_b, ffn_w_gate, ffn_w_up, ffn_w_down, moe_w_router, moe_b_router, moe_w_gate, moe_w_up, moe_w_down, ple_w_gate, ple_w_proj):
    n_pb, p_seq, _ = x_prompt.shape
    n_sb, s_seq, _ = x_sample.shape
    n_p, n_s = n_pb * p_seq, n_sb * s_seq
    vec = lambda v: v.reshape(1, -1).astype(F32)

    xp = x_prompt.reshape(n_p, D_MODEL)
    xs = x_sample.reshape(n_s, D_MODEL)
    w1 = conv_w_pw1[0].astype(BF16)
    b1 = vec(conv_b_pw1[0])
    u_p = _pw1_glu(xp, w1, b1)
    u_s = _pw1_glu(xs, w1, b1)
    w_dw, b_dw = conv_w_dw[0].astype(F32), vec(conv_b_dw[0])
    state = jnp.pad(state_conv[0], ((0, 0), (CONV_HALO - (CONV_WIDTH - 1), 0), (0, 0))).reshape(n_sb * CONV_HALO, D_MODEL)
    yc_p = _dwconv(u_p, u_p, w_dw, b_dw, n_seq=n_pb, seq=p_seq, bt=256, zero_first=True)
    yc_s = _dwconv(u_s, state, w_dw, b_dw, n_seq=n_sb, seq=s_seq, bt=s_seq, zero_first=False)
    w2 = conv_w_pw2[0].astype(BF16)
    tail0 = (w2, vec(conv_b_pw2[0]), vec(conv_ln_g[0]), vec(conv_ln_b[0]), vec(ln1_g[0]), vec(ln1_b[0]))
    h, hb = _pw2_ln1(yc_p, yc_s, xp, xs, *tail0)
    f = _ffn(hb, ffn_w_gate[0].astype(BF16), ffn_w_up[0].astype(BF16), ffn_w_down[0].astype(BF16))
    x1, x1b = _ple_ln2(h, hb, f, p_prompt[0].reshape(n_p, PLE_DIM), p_sample[0].reshape(n_s, PLE_DIM),
                       ple_w_gate[0].astype(BF16), ple_w_proj[0].astype(BF16), vec(ln2_g[0]), vec(ln2_b[0]))

    q = _mm(x1b, attn_w_q[0].astype(BF16), scale=HEAD_DIM ** -0.5, out_dtype=BF16)
    kv = _mm(x1b, w_kv.astype(BF16), scale=1.0, out_dtype=F32)
    bias = _rel_bias(rel_bias)
    sinks = jnp.repeat(attn_sinks[0].astype(F32), CHUNK).reshape(N_KV_HEADS, GROUP * CHUNK, 1)
    o_p = _attention(q, kv, kv, kv, bias, sinks, n_seq=n_pb, seq=p_seq, tq=256, row0=0, mask_first=True)
    o_s = _attention(q, kv, cache_k.reshape(n_sb * WINDOW, KV_DIM), cache_v.reshape(n_sb * WINDOW, KV_DIM),
                     bias, sinks, n_seq=n_sb, seq=s_seq, tq=s_seq, row0=n_p, mask_first=False)
    h1, h1p, route = _oproj_ln1_route(o_p, o_s, x1, attn_w_o[0].astype(BF16), vec(ln1_g[1]), vec(ln1_b[1]),
                                      moe_w_router[0], moe_b_router[0])

    dest, sorted_tok, block_base, block_e, sub_block_e, sub_valid = _route_plan(route)
    xg = _gather_rows(sorted_tok, block_base, h1p)
    a = _moe_up(block_e, sub_valid, xg, moe_w_gate[0], moe_w_up[0])
    ys = _moe_down(sub_block_e, sub_valid, a, moe_w_down[0])
    tail1 = (ple_w_gate[1].astype(BF16), ple_w_proj[1].astype(BF16), vec(ln2_g[1]), vec(ln2_b[1]))
    y_p = _combine_ple_ln2(dest, h1, route, p_prompt[1].reshape(n_p, PLE_DIM), ys, *tail1, row0=0, n_rows=n_p)
    y_s = _combine_ple_ln2(dest, h1, route, p_sample[1].reshape(n_s, PLE_DIM), ys, *tail1, row0=n_p, n_rows=n_s)

    keep = CONV_WIDTH - 1
    conv_p = u_p.reshape(n_pb, p_seq, D_MODEL)[None, :, p_seq - keep:]
    conv_s = u_s.reshape(n_sb, s_seq, D_MODEL)[None, :, s_seq - keep:]
    wps = p_seq // WINDOW
    kv_p = kv.reshape(-1, WINDOW, 2 * KV_DIM)[wps - 1:n_pb * wps:wps].reshape(n_pb, WINDOW, 2, N_KV_HEADS, HEAD_DIM)
    kv_s = kv[n_p:].reshape(n_sb, s_seq, 2, N_KV_HEADS, HEAD_DIM)
    k_s = jnp.concatenate([cache_k, kv_s[:, :, 0]], axis=1)[:, s_seq:]
    v_s = jnp.concatenate([cache_v, kv_s[:, :, 1]], axis=1)[:, s_seq:]
    return (y_p.reshape(n_pb, p_seq, D_MODEL), y_s.reshape(n_sb, s_seq, D_MODEL), conv_p, conv_s,
            kv_p[:, :, 0], kv_p[:, :, 1], k_s, v_s)
```

```python
import functools
import math

import jax
import jax.numpy as jnp
from jax import lax
from jax.experimental import pallas as pl
from jax.experimental.pallas import tpu as pltpu

D_MODEL = 2048
CHUNK = 64
CONV_WIDTH = 31
HEAD_DIM = 64
N_HEADS = 32
N_KV_HEADS = 8
GROUP = N_HEADS // N_KV_HEADS
WINDOW = 128
BAND = WINDOW + CHUNK
NUM_BUCKETS = 32
MAX_DISTANCE = 128
D_FF = 5632
N_EXPERTS = 8
TOP_K = 2
PLE_DIM = 256
DEPTH = 2
ALPHA = (2 * DEPTH) ** 0.25
LN_EPS = 1e-5
NEG_INF = -1e30

KV_DIM = N_KV_HEADS * HEAD_DIM
CONV_HALO = 32
MOE_ROWS = 1024
MOE_SUB_ROWS = 512
LANES = 128
SUBLANES = 8
PACKED_TILES = D_MODEL // 2 // LANES
DMA_ISSUE_UNROLL = 8
DMA_PRIORITIES = 2
MIB = 1024 * 1024

F32 = jnp.float32
BF16 = jnp.bfloat16


def _params(n_axes, vmem_mib):
    return pltpu.CompilerParams(dimension_semantics=("arbitrary",) * n_axes,
                                vmem_limit_bytes=vmem_mib * MIB)


def _dot(a, b):
    return jnp.dot(a, b, preferred_element_type=F32)


def _ln(x, g, b):
    mu = jnp.mean(x, axis=-1, keepdims=True)
    xc = x - mu
    var = jnp.mean(xc * xc, axis=-1, keepdims=True)
    return xc * lax.rsqrt(var + LN_EPS) * g + b


def _silu(x):
    return x * jax.nn.sigmoid(x)


def _pw1_glu_kernel(x_ref, wa_ref, wb_ref, ba_ref, bb_ref, u_ref, xb_ref):
    @pl.when(pl.program_id(1) == 0)
    def _():
        xb_ref[...] = x_ref[...].astype(BF16)

    xb = xb_ref[...]
    a = _dot(xb, wa_ref[...]) + ba_ref[...]
    g = _dot(xb, wb_ref[...]) + bb_ref[...]
    u_ref[...] = a * jax.nn.sigmoid(g)


def _pw1_glu(x, w, b):
    n = x.shape[0]
    bm, bn = math.gcd(n, 1024), 512
    nj = D_MODEL // bn
    return pl.pallas_call(
        _pw1_glu_kernel,
        grid=(n // bm, nj),
        in_specs=[pl.BlockSpec((bm, D_MODEL), lambda i, j: (i, 0)),
                  pl.BlockSpec((D_MODEL, bn), lambda i, j: (0, j)),
                  pl.BlockSpec((D_MODEL, bn), lambda i, j: (0, j + nj)),
                  pl.BlockSpec((1, bn), lambda i, j: (0, j)),
                  pl.BlockSpec((1, bn), lambda i, j: (0, j + nj))],
        out_specs=pl.BlockSpec((bm, bn), lambda i, j: (i, j)),
        out_shape=jax.ShapeDtypeStruct((n, D_MODEL), F32),
        scratch_shapes=[pltpu.VMEM((bm, D_MODEL), BF16)],
        compiler_params=_params(2, 48),
    )(x, w, w, b, b)


def _dwconv_kernel(prev_ref, cur_ref, w_ref, b_ref, y_ref, e_ref, *, bt, zero_first, rc):
    prev = prev_ref[...]
    if zero_first:
        prev = jnp.where(pl.program_id(1) == 0, 0.0, prev)
    e_ref[0, 0:CONV_HALO, :] = prev
    e_ref[0, CONV_HALO:, :] = cur_ref[...]
    e0 = e_ref[0]
    n_e = bt + CONV_HALO
    for rho in range(1, SUBLANES):
        e_ref[rho] = pltpu.roll(e0, n_e - rho, axis=0)
    off = CONV_HALO - (CONV_WIDTH - 1)
    for r0 in range(0, bt, rc):
        acc = jnp.broadcast_to(b_ref[...], (rc, b_ref.shape[1]))
        for k in range(CONV_WIDTH):
            rho = (k + off) % SUBLANES
            base = r0 + k + off - rho
            acc = acc + e_ref[rho, base:base + rc, :] * w_ref[k:k + 1, :]
        y_ref[r0:r0 + rc, :] = acc


def _dwconv(u, prev_src, w, b, *, n_seq, seq, bt, zero_first):
    bc = 512
    nt = seq // bt
    hb = bt // CONV_HALO
    if zero_first:
        prev_map = lambda s, t, c: (jnp.maximum((s * nt + t) * hb - 1, 0), c)
    else:
        prev_map = lambda s, t, c: (s, c)
    return pl.pallas_call(
        functools.partial(_dwconv_kernel, bt=bt, zero_first=zero_first, rc=32),
        grid=(n_seq, nt, D_MODEL // bc),
        in_specs=[pl.BlockSpec((CONV_HALO, bc), prev_map),
                  pl.BlockSpec((bt, bc), lambda s, t, c: (s * nt + t, c)),
                  pl.BlockSpec((CONV_WIDTH, bc), lambda s, t, c: (0, c)),
                  pl.BlockSpec((1, bc), lambda s, t, c: (0, c))],
        out_specs=pl.BlockSpec((bt, bc), lambda s, t, c: (s * nt + t, c)),
        out_shape=jax.ShapeDtypeStruct(u.shape, F32),
        scratch_shapes=[pltpu.VMEM((SUBLANES, bt + CONV_HALO, bc), F32)],
        compiler_params=_params(3, 32),
    )(prev_src, u, w, b)


def _two_group_specs(bm, width, n_first):
    nb = n_first // bm
    return (pl.BlockSpec((bm, width), lambda i, *_: (jnp.minimum(i, nb - 1), 0)),
            pl.BlockSpec((bm, width), lambda i, *_: (jnp.maximum(i - nb, 0), 0)))


def _pick_group(first_ref, second_ref, n_first_blocks):
    return jnp.where(pl.program_id(0) < n_first_blocks, first_ref[...], second_ref[...])


def _pw2_ln1_kernel(yp_ref, ys_ref, xp_ref, xs_ref, w_ref, b2_ref, cg_ref, cb_ref, g_ref, b_ref, h_ref, hb_ref,
                    *, nb_first):
    y = _silu(_ln(_pick_group(yp_ref, ys_ref, nb_first), cg_ref[...], cb_ref[...]))
    mix = _dot(y.astype(BF16), w_ref[...]) + b2_ref[...]
    h = _ln(ALPHA * _pick_group(xp_ref, xs_ref, nb_first) + mix, g_ref[...], b_ref[...])
    h_ref[...] = h
    hb_ref[...] = h.astype(BF16)


def _pw2_ln1(y_p, y_s, x_p, x_s, w, b2, cg, cb, g, b):
    n_p, n = y_p.shape[0], y_p.shape[0] + y_s.shape[0]
    bm = 256
    row = pl.BlockSpec((bm, D_MODEL), lambda i: (i, 0))
    vec = pl.BlockSpec((1, D_MODEL), lambda i: (0, 0))
    return pl.pallas_call(
        functools.partial(_pw2_ln1_kernel, nb_first=n_p // bm),
        grid=(n // bm,),
        in_specs=[*_two_group_specs(bm, D_MODEL, n_p), *_two_group_specs(bm, D_MODEL, n_p),
                  pl.BlockSpec((D_MODEL, D_MODEL), lambda i: (0, 0)), vec, vec, vec, vec, vec],
        out_specs=[row, row],
        out_shape=[jax.ShapeDtypeStruct((n, D_MODEL), F32), jax.ShapeDtypeStruct((n, D_MODEL), BF16)],
        compiler_params=_params(1, 56),
    )(y_p, y_s, x_p, x_s, w, b2, cg, cb, g, b)


def _ffn_kernel(x_ref, wg_ref, wu_ref, wd_ref, o_ref):
    j = pl.program_id(1)
    x = x_ref[...]
    g = _dot(x, wg_ref[...])
    u = _dot(x, wu_ref[...])
    part = _dot((_silu(g) * u).astype(BF16), wd_ref[...])

    @pl.when(j == 0)
    def _():
        o_ref[...] = part

    @pl.when(j > 0)
    def _():
        o_ref[...] += part


def _ffn(xb, wg, wu, wd):
    n = xb.shape[0]
    bm, bf = math.gcd(n, 1024), 512
    return pl.pallas_call(
        _ffn_kernel,
        grid=(n // bm, D_FF // bf),
        in_specs=[pl.BlockSpec((bm, D_MODEL), lambda i, j: (i, 0)),
                  pl.BlockSpec((D_MODEL, bf), lambda i, j: (0, j)),
                  pl.BlockSpec((D_MODEL, bf), lambda i, j: (0, j)),
                  pl.BlockSpec((bf, D_MODEL), lambda i, j: (j, 0))],
        out_specs=pl.BlockSpec((bm, D_MODEL), lambda i, j: (i, 0)),
        out_shape=jax.ShapeDtypeStruct((n, D_MODEL), F32),
        compiler_params=_params(2, 60),
    )(xb, wg, wu, wd)


def _ple_ln2_kernel(h_ref, hb_ref, f_ref, pp_ref, ps_ref, wg_ref, wp_ref, g_ref, b_ref, x_ref, xb_ref, *, nb_first):
    gate = jax.nn.sigmoid(_dot(hb_ref[...], wg_ref[...]))
    proj = _dot(_pick_group(pp_ref, ps_ref, nb_first).astype(BF16), wp_ref[...])
    x = _ln(ALPHA * h_ref[...] + f_ref[...] + gate * proj, g_ref[...], b_ref[...])
    x_ref[...] = x
    xb_ref[...] = x.astype(BF16)


def _ple_ln2(h, hb, f, p_p, p_s, wg, wp, g, b):
    n = h.shape[0]
    bm = math.gcd(math.gcd(n, p_p.shape[0]), 512)
    row = pl.BlockSpec((bm, D_MODEL), lambda i: (i, 0))
    vec = pl.BlockSpec((1, D_MODEL), lambda i: (0, 0))
    return pl.pallas_call(
        functools.partial(_ple_ln2_kernel, nb_first=p_p.shape[0] // bm),
        grid=(n // bm,),
        in_specs=[row, row, row, *_two_group_specs(bm, PLE_DIM, p_p.shape[0]),
                  pl.BlockSpec((D_MODEL, D_MODEL), lambda i: (0, 0), pipeline_mode=pl.Buffered(1)),
                  pl.BlockSpec((PLE_DIM, D_MODEL), lambda i: (0, 0)), vec, vec],
        out_specs=[row, row],
        out_shape=[jax.ShapeDtypeStruct((n, D_MODEL), F32), jax.ShapeDtypeStruct((n, D_MODEL), BF16)],
        compiler_params=_params(1, 56),
    )(h, hb, f, p_p, p_s, wg, wp, g, b)


def _mm_kernel(x_ref, w_ref, o_ref, *, scale):
    acc = _dot(x_ref[...], w_ref[...])
    if scale != 1.0:
        acc = acc * scale
    o_ref[...] = acc.astype(o_ref.dtype)


def _mm(xb, w, *, scale, out_dtype):
    n, k = xb.shape
    m = w.shape[1]
    bm, bn = math.gcd(n, 1024), 512
    return pl.pallas_call(
        functools.partial(_mm_kernel, scale=scale),
        grid=(n // bm, m // bn),
        in_specs=[pl.BlockSpec((bm, k), lambda i, j: (i, 0)),
                  pl.BlockSpec((k, bn), lambda i, j: (0, j))],
        out_specs=pl.BlockSpec((bm, bn), lambda i, j: (i, j)),
        out_shape=jax.ShapeDtypeStruct((n, m), out_dtype),
        compiler_params=_params(2, 32),
    )(xb, w)


def _t5_bucket(rel):
    half = NUM_BUCKETS // 2
    exact = half // 2
    ret = jnp.where(rel > 0, half, 0)
    n = jnp.abs(rel)
    large = exact + (jnp.log(jnp.maximum(n, 1).astype(F32) / exact)
                     / math.log(MAX_DISTANCE / exact) * (half - exact)).astype(jnp.int32)
    large = jnp.minimum(large, half - 1)
    return ret + jnp.where(n < exact, n, large)


def _bias_kernel(table_ref, bucket_ref, o_ref):
    bk = bucket_ref[...]
    masks = [bk == b for b in range(NUM_BUCKETS)]
    for h in range(N_HEADS):
        acc = jnp.zeros(bk.shape, F32)
        for b in range(NUM_BUCKETS):
            acc = jnp.where(masks[b], table_ref[b, h], acc)
        o_ref[h] = acc


def _rel_bias(table):
    i = jnp.arange(CHUNK)[:, None]
    j = jnp.arange(BAND)[None, :]
    bucket = _t5_bucket((j - WINDOW) - i).astype(jnp.int32)
    out = pl.pallas_call(
        _bias_kernel,
        in_specs=[pl.BlockSpec(memory_space=pltpu.SMEM), pl.BlockSpec(memory_space=pltpu.VMEM)],
        out_specs=pl.BlockSpec(memory_space=pltpu.VMEM),
        out_shape=jax.ShapeDtypeStruct((N_HEADS, CHUNK, BAND), F32),
    )(table.astype(F32), bucket)
    return out.reshape(N_KV_HEADS, GROUP * CHUNK, BAND)


def _attn_kernel(q_ref, kp_ref, vp_ref, kc_ref, vc_ref, bias_ref, sink_ref, o_ref, q_scr, k_scr, v_scr,
                 *, tq, mask_first):
    t = pl.program_id(1)
    for h in range(N_KV_HEADS):
        hs = slice(h * HEAD_DIM, (h + 1) * HEAD_DIM)
        k_scr[h, 0:WINDOW, :] = kp_ref[:, hs].astype(BF16)
        k_scr[h, WINDOW:, :] = kc_ref[:, hs].astype(BF16)
        v_scr[h, 0:WINDOW, :] = vp_ref[:, hs].astype(BF16)
        v_scr[h, WINDOW:, :] = vc_ref[:, hs].astype(BF16)
    for h in range(N_HEADS):
        q_scr[h] = q_ref[:, h * HEAD_DIM:(h + 1) * HEAD_DIM]
    rows = GROUP * CHUNK
    bias = bias_ref[...]
    sink = sink_ref[...]
    n_chunks = tq // CHUNK
    scores = []
    for c in range(n_chunks):
        cs = slice(c * CHUNK, (c + 1) * CHUNK)
        ks = slice(c * CHUNK, c * CHUNK + BAND)
        qh = q_scr[:, cs, :].reshape(N_KV_HEADS, rows, HEAD_DIM)
        scores.append(jnp.einsum('hqd,hkd->hqk', qh, k_scr[:, ks, :], preferred_element_type=F32))
    probs = []
    for c in range(n_chunks):
        s = scores[c] + bias
        if mask_first and c * CHUNK < WINDOW:
            first_valid = WINDOW - (t * tq + c * CHUNK)
            kidx = lax.broadcasted_iota(jnp.int32, s.shape, 2)
            s = jnp.where(kidx >= first_valid, s, NEG_INF)
        m = jnp.maximum(jnp.max(s, axis=-1, keepdims=True), sink)
        e = jnp.exp(s - m)
        den = jnp.sum(e, axis=-1, keepdims=True) + jnp.exp(sink - m)
        probs.append((e * (1.0 / den)).astype(BF16))
    for c in range(n_chunks):
        cs = slice(c * CHUNK, (c + 1) * CHUNK)
        ks = slice(c * CHUNK, c * CHUNK + BAND)
        o = jnp.einsum('hqk,hkd->hqd', probs[c], v_scr[:, ks, :], preferred_element_type=F32)
        o = o.reshape(N_HEADS, CHUNK, HEAD_DIM).astype(o_ref.dtype)
        for h in range(N_HEADS):
            o_ref[cs, h * HEAD_DIM:(h + 1) * HEAD_DIM] = o[h]


def _attention(q, kv, k_prev, v_prev, bias, sinks, *, n_seq, seq, tq, row0, mask_first):
    nt = seq // tq
    qb0 = row0 // tq
    if mask_first:
        wpt = tq // WINDOW
        kp_spec = pl.BlockSpec((WINDOW, KV_DIM), lambda s, t: (jnp.maximum((s * nt + t) * wpt - 1, 0), 0))
        vp_spec = pl.BlockSpec((WINDOW, KV_DIM), lambda s, t: (jnp.maximum((s * nt + t) * wpt - 1, 0), 1))
    else:
        kp_spec = pl.BlockSpec((WINDOW, KV_DIM), lambda s, t: (s, 0))
        vp_spec = pl.BlockSpec((WINDOW, KV_DIM), lambda s, t: (s, 0))
    return pl.pallas_call(
        functools.partial(_attn_kernel, tq=tq, mask_first=mask_first),
        grid=(n_seq, nt),
        in_specs=[pl.BlockSpec((tq, D_MODEL), lambda s, t: (qb0 + s * nt + t, 0)),
                  kp_spec, vp_spec,
                  pl.BlockSpec((tq, KV_DIM), lambda s, t: (qb0 + s * nt + t, 0)),
                  pl.BlockSpec((tq, KV_DIM), lambda s, t: (qb0 + s * nt + t, 1)),
                  pl.BlockSpec((N_KV_HEADS, GROUP * CHUNK, BAND), lambda s, t: (0, 0, 0)),
                  pl.BlockSpec((N_KV_HEADS, GROUP * CHUNK, 1), lambda s, t: (0, 0, 0))],
        out_specs=pl.BlockSpec((tq, D_MODEL), lambda s, t: (s * nt + t, 0)),
        out_shape=jax.ShapeDtypeStruct((n_seq * seq, D_MODEL), BF16),
        scratch_shapes=[pltpu.VMEM((N_HEADS, tq, HEAD_DIM), BF16),
                        pltpu.VMEM((N_KV_HEADS, WINDOW + tq, HEAD_DIM), BF16),
                        pltpu.VMEM((N_KV_HEADS, WINDOW + tq, HEAD_DIM), BF16)],
        compiler_params=_params(2, 40),
    )(q, k_prev, v_prev, kv, kv, bias, sinks)


def _f32_bits(x):
    return lax.bitcast_convert_type(x, jnp.uint32)


def _pack_bf16_pairs(x):
    c = x.shape[1] // 2
    lo = _f32_bits(x[:, :c].astype(BF16).astype(F32))
    hi = _f32_bits(x[:, c:].astype(BF16).astype(F32))
    return hi | (lo >> 16)


def _unpack_bf16_pairs(w):
    lo = lax.bitcast_convert_type(w << 16, F32).astype(BF16)
    hi = lax.bitcast_convert_type(w & jnp.uint32(0xFFFF0000), F32).astype(BF16)
    return lo, hi


def _route(h, wh, wl, b):
    hh = h.astype(BF16)
    hl = (h - hh.astype(F32)).astype(BF16)
    logits = _dot(hh, wh) + (_dot(hl, wh) + _dot(hh, wl)) + b
    lane = lax.broadcasted_iota(jnp.int32, logits.shape, 1).astype(F32)
    ninf = -jnp.inf
    lg = jnp.where(lane < N_EXPERTS, logits, ninf)
    m1 = jnp.max(lg, axis=-1, keepdims=True)
    i1 = jnp.min(jnp.where(lg == m1, lane, float(LANES)), axis=-1, keepdims=True)
    lg2 = jnp.where(lane == i1, ninf, lg)
    m2 = jnp.max(lg2, axis=-1, keepdims=True)
    i2 = jnp.min(jnp.where(lg2 == m2, lane, float(LANES)), axis=-1, keepdims=True)
    ex = jnp.exp(m2 - m1)
    den = 1.0 + ex
    out = jnp.where(lane == 0, i1, jnp.where(lane == 1, i2, jnp.where(lane == 2, 1.0 / den, ex / den)))
    return jnp.where(lane < 4, out, 0.0)


def _oproj_ln1_kernel(op_ref, os_ref, x_ref, w_ref, g_ref, b_ref, rwh_ref, rwl_ref, rb_ref, h_ref, hp_ref, r_ref,
                      *, nb_first):
    mix = _dot(_pick_group(op_ref, os_ref, nb_first), w_ref[...])
    h = _ln(ALPHA * x_ref[...] + mix, g_ref[...], b_ref[...])
    h_ref[...] = h
    r_ref[...] = _route(h, rwh_ref[...], rwl_ref[...], rb_ref[...])
    packed = _pack_bf16_pairs(h)
    for j in range(PACKED_TILES):
        hp_ref[pl.ds(j, h.shape[0], stride=PACKED_TILES), :] = packed[:, j * LANES:(j + 1) * LANES]


def _oproj_ln1_route(o_p, o_s, x, w, g, b, w_router, b_router):
    n = x.shape[0]
    bm = 256
    row = pl.BlockSpec((bm, D_MODEL), lambda i: (i, 0))
    vec = pl.BlockSpec((1, D_MODEL), lambda i: (0, 0))
    rw = pl.BlockSpec((D_MODEL, LANES), lambda i: (0, 0))
    wp = jnp.zeros((D_MODEL, LANES), F32).at[:, :N_EXPERTS].set(w_router.astype(F32))
    wh = wp.astype(BF16)
    wl = (wp - wh.astype(F32)).astype(BF16)
    bp = jnp.zeros((1, LANES), F32).at[0, :N_EXPERTS].set(b_router.astype(F32))
    return pl.pallas_call(
        functools.partial(_oproj_ln1_kernel, nb_first=o_p.shape[0] // bm),
        grid=(n // bm,),
        in_specs=[*_two_group_specs(bm, D_MODEL, o_p.shape[0]), row,
                  pl.BlockSpec((D_MODEL, D_MODEL), lambda i: (0, 0)), vec, vec,
                  rw, rw, pl.BlockSpec((1, LANES), lambda i: (0, 0))],
        out_specs=[row, pl.BlockSpec((bm * PACKED_TILES, LANES), lambda i: (i, 0)),
                   pl.BlockSpec((bm, LANES), lambda i: (i, 0))],
        out_shape=[jax.ShapeDtypeStruct((n, D_MODEL), F32),
                   jax.ShapeDtypeStruct((n * PACKED_TILES, LANES), jnp.uint32),
                   jax.ShapeDtypeStruct((n, LANES), F32)],
        compiler_params=_params(1, 48),
    )(o_p, o_s, x, w, g, b, wh, wl, bp)


def _gather_kernel(tok_ref, base_ref, h_hbm, o_ref, buf, sem, *, rows, n_blocks):
    i = pl.program_id(0)

    def row_copy(tok, slot, r):
        src = pl.multiple_of(tok * PACKED_TILES, PACKED_TILES)
        dst = pl.multiple_of(r * PACKED_TILES, PACKED_TILES)
        return pltpu.make_async_copy(h_hbm.at[pl.ds(src, PACKED_TILES)], buf.at[slot, pl.ds(dst, PACKED_TILES)],
                                     sem.at[slot])

    def issue(blk, slot):
        base = base_ref[blk]

        def body(r8, c):
            for q in range(DMA_ISSUE_UNROLL):
                r = r8 * DMA_ISSUE_UNROLL + q
                row_copy(tok_ref[base + r], slot, r).start(priority=q % DMA_PRIORITIES)
            return c
        lax.fori_loop(0, rows // DMA_ISSUE_UNROLL, body, 0)

    @pl.when(i == 0)
    def _():
        issue(0, 0)

    @pl.when(i + 1 < n_blocks)
    def _():
        issue(i + 1, (i + 1) % 2)

    slot = i % 2
    pltpu.make_async_copy(h_hbm.at[pl.ds(0, rows * PACKED_TILES)], buf.at[slot], sem.at[slot]).wait()
    half = D_MODEL // 2
    for j in range(PACKED_TILES):
        lo, hi = _unpack_bf16_pairs(buf[slot, pl.ds(j, rows, stride=PACKED_TILES), :])
        o_ref[:, j * LANES:(j + 1) * LANES] = lo
        o_ref[:, half + j * LANES:half + (j + 1) * LANES] = hi


def _gather_rows(sorted_tok, block_base, hp):
    rows = MOE_SUB_ROWS
    n_blocks = block_base.shape[0]
    return pl.pallas_call(
        functools.partial(_gather_kernel, rows=rows, n_blocks=n_blocks),
        grid_spec=pltpu.PrefetchScalarGridSpec(
            num_scalar_prefetch=2,
            grid=(n_blocks,),
            in_specs=[pl.BlockSpec(memory_space=pl.ANY)],
            out_specs=pl.BlockSpec((rows, D_MODEL), lambda i, tok, base: (i, 0)),
            scratch_shapes=[pltpu.VMEM((2, rows * PACKED_TILES, LANES), jnp.uint32),
                            pltpu.SemaphoreType.DMA((2,))]),
        out_shape=jax.ShapeDtypeStruct((n_blocks * rows, D_MODEL), BF16),
        compiler_params=_params(1, 32),
    )(sorted_tok, block_base, hp)


def _expert_changed(be_ref, i):
    return jnp.logical_or(i == 0, be_ref[i] != be_ref[jnp.maximum(i - 1, 0)])


def _moe_up_kernel(be_ref, sv_ref, ws_ref, x_ref, wg_ref, wu_ref, a_ref, wgb_ref, wub_ref, *, n_blocks):
    i = pl.program_id(1)
    sub = MOE_ROWS // MOE_SUB_ROWS
    nxt = jnp.minimum(i, n_blocks - 1)
    stage = jnp.logical_and(jnp.logical_and(i < n_blocks, sv_ref[sub * nxt] > 0), _expert_changed(be_ref, nxt))

    @pl.when(stage)
    def _():
        s = ws_ref[nxt]
        wgb_ref[s] = wg_ref[...].astype(BF16)
        wub_ref[s] = wu_ref[...].astype(BF16)

    b = jnp.maximum(i - 1, 0)
    live = i > 0
    used = jnp.logical_and(live, sv_ref[sub * b] > 0)
    full = jnp.logical_and(live, sv_ref[sub * b + sub - 1] > 0)

    def swiglu(x):
        s = ws_ref[b]
        g = _dot(x, wgb_ref[s])
        u = _dot(x, wub_ref[s])
        return (_silu(g) * u).astype(a_ref.dtype)

    @pl.when(full)
    def _():
        a_ref[...] = swiglu(x_ref[...])

    @pl.when(jnp.logical_and(used, jnp.logical_not(full)))
    def _():
        a_ref[0:MOE_SUB_ROWS, :] = swiglu(x_ref[0:MOE_SUB_ROWS, :])
        a_ref[MOE_SUB_ROWS:, :] = jnp.zeros((MOE_ROWS - MOE_SUB_ROWS, a_ref.shape[1]), a_ref.dtype)

    @pl.when(jnp.logical_and(live, jnp.logical_not(used)))
    def _():
        a_ref[...] = jnp.zeros(a_ref.shape, a_ref.dtype)


def _moe_up(block_e, sub_valid, xs, wg, wu):
    assert MOE_ROWS == 2 * MOE_SUB_ROWS
    cap = xs.shape[0]
    bf = 512
    nb = cap // MOE_ROWS
    changes = jnp.concatenate([jnp.zeros((1,), jnp.int32), (block_e[1:] != block_e[:-1]).astype(jnp.int32)])
    w_slot = (jnp.cumsum(changes) % 2).astype(jnp.int32)
    row = lambda j, i, be, sv, ws: (jnp.maximum(i - 1, 0), 0)
    wmap = lambda j, i, be, sv, ws: (be[jnp.minimum(i, nb - 1)], 0, j)
    return pl.pallas_call(
        functools.partial(_moe_up_kernel, n_blocks=nb),
        grid_spec=pltpu.PrefetchScalarGridSpec(
            num_scalar_prefetch=3,
            grid=(D_FF // bf, nb + 1),
            in_specs=[pl.BlockSpec((MOE_ROWS, D_MODEL), row),
                      pl.BlockSpec((None, D_MODEL, bf), wmap),
                      pl.BlockSpec((None, D_MODEL, bf), wmap)],
            out_specs=pl.BlockSpec((MOE_ROWS, bf), lambda j, i, be, sv, ws: (jnp.maximum(i - 1, 0), j)),
            scratch_shapes=[pltpu.VMEM((2, D_MODEL, bf), BF16), pltpu.VMEM((2, D_MODEL, bf), BF16)]),
        out_shape=jax.ShapeDtypeStruct((cap, D_FF), BF16),
        compiler_params=_params(2, 56),
    )(block_e, sub_valid, w_slot, xs, wg, wu)


def _moe_down_kernel(be_ref, sv_ref, a_ref, wd_ref, y_ref, wdb_ref):
    i = pl.program_id(1)
    used = sv_ref[i] > 0

    @pl.when(jnp.logical_and(used, _expert_changed(be_ref, i)))
    def _():
        wdb_ref[...] = wd_ref[...].astype(BF16)

    @pl.when(used)
    def _():
        y_ref[...] = _dot(a_ref[...], wdb_ref[...])

    @pl.when(jnp.logical_not(used))
    def _():
        y_ref[...] = jnp.zeros(y_ref.shape, y_ref.dtype)


def _moe_down(block_e, sub_valid, a, wd):
    cap = a.shape[0]
    bn = 512
    return pl.pallas_call(
        _moe_down_kernel,
        grid_spec=pltpu.PrefetchScalarGridSpec(
            num_scalar_prefetch=2,
            grid=(D_MODEL // bn, cap // MOE_SUB_ROWS),
            in_specs=[pl.BlockSpec((MOE_SUB_ROWS, D_FF), lambda j, i, be, nu: (i, 0)),
                      pl.BlockSpec((None, D_FF, bn), lambda j, i, be, nu: (be[i], 0, j))],
            out_specs=pl.BlockSpec((MOE_SUB_ROWS, bn), lambda j, i, be, nu: (i, j)),
            scratch_shapes=[pltpu.VMEM((D_FF, bn), BF16)]),
        out_shape=jax.ShapeDtypeStruct((cap, D_MODEL), F32),
        compiler_params=_params(2, 56),
    )(block_e, sub_valid, a, wd)


def _combine_kernel(dest_ref, h_ref, r_ref, p_ref, wg_ref, wp_ref, g_ref, b_ref, ys_hbm, y_ref, buf, sem,
                    *, bm, row0, n_blocks):
    i = pl.program_id(0)

    def row_copy(src, slot, k, r):
        return pltpu.make_async_copy(ys_hbm.at[pl.ds(src, 1)], buf.at[slot, k, pl.ds(r, 1)], sem.at[slot])

    def issue(blk, slot):
        def body(r4, c):
            for q in range(DMA_ISSUE_UNROLL // TOP_K):
                r = r4 * (DMA_ISSUE_UNROLL // TOP_K) + q
                a = TOP_K * (row0 + blk * bm + r)
                for k in range(TOP_K):
                    row_copy(dest_ref[a + k], slot, k, r).start(priority=k % DMA_PRIORITIES)
            return c
        lax.fori_loop(0, bm // (DMA_ISSUE_UNROLL // TOP_K), body, 0)

    @pl.when(i == 0)
    def _():
        issue(0, 0)

    @pl.when(i + 1 < n_blocks)
    def _():
        issue(i + 1, (i + 1) % 2)

    slot = i % 2
    for k in range(TOP_K):
        pltpu.make_async_copy(ys_hbm.at[pl.ds(0, bm)], buf.at[slot, k], sem.at[slot]).wait()

    h = h_ref[...]
    r = r_ref[...]
    f = buf[slot, 0] * r[:, 2:3] + buf[slot, 1] * r[:, 3:4]
    gate = jax.nn.sigmoid(_dot(h.astype(BF16), wg_ref[...]))
    proj = _dot(p_ref[...].astype(BF16), wp_ref[...])
    y_ref[...] = _ln(ALPHA * h + f + gate * proj, g_ref[...], b_ref[...])


def _combine_ple_ln2(dest, h, route, p, ys, wg, wp, g, b, *, row0, n_rows):
    bm = 256
    n_blocks = n_rows // bm
    b0 = row0 // bm
    vec = pl.BlockSpec((1, D_MODEL), lambda i, d: (0, 0))
    return pl.pallas_call(
        functools.partial(_combine_kernel, bm=bm, row0=row0, n_blocks=n_blocks),
        grid_spec=pltpu.PrefetchScalarGridSpec(
            num_scalar_prefetch=1,
            grid=(n_blocks,),
            in_specs=[pl.BlockSpec((bm, D_MODEL), lambda i, d: (b0 + i, 0)),
                      pl.BlockSpec((bm, LANES), lambda i, d: (b0 + i, 0)),
                      pl.BlockSpec((bm, PLE_DIM), lambda i, d: (i, 0)),
                      pl.BlockSpec((D_MODEL, D_MODEL), lambda i, d: (0, 0)),
                      pl.BlockSpec((PLE_DIM, D_MODEL), lambda i, d: (0, 0)),
                      vec, vec,
                      pl.BlockSpec(memory_space=pl.ANY)],
            out_specs=pl.BlockSpec((bm, D_MODEL), lambda i, d: (i, 0)),
            scratch_shapes=[pltpu.VMEM((2, TOP_K, bm, D_MODEL), F32), pltpu.SemaphoreType.DMA((2,))]),
        out_shape=jax.ShapeDtypeStruct((n_rows, D_MODEL), F32),
        compiler_params=_params(1, 48),
    )(dest, h, route, p, wg, wp, g, b, ys)


def _route_plan(route):
    n = route.shape[0]
    n_assign = n * TOP_K
    flat_e = route[:, :TOP_K].astype(jnp.int32).reshape(n_assign)
    onehot = (flat_e[:, None] == jnp.arange(N_EXPERTS, dtype=jnp.int32)[None, :]).astype(jnp.int32)
    csum = jnp.cumsum(onehot, axis=0)
    counts = csum[-1]
    rank = jnp.sum(csum * onehot, axis=1) - 1
    padded = (counts + MOE_ROWS - 1) // MOE_ROWS * MOE_ROWS
    pad_end = jnp.cumsum(padded)
    pad_start = jnp.sum(onehot * (pad_end - padded)[None, :], axis=1)
    dest = (pad_start + rank).astype(jnp.int32)
    n_blocks = -(-n_assign // MOE_ROWS) + N_EXPERTS - 1
    block_start = jnp.arange(n_blocks, dtype=pad_end.dtype) * MOE_ROWS
    block_e = jnp.minimum(jnp.sum(pad_end[None, :] <= block_start[:, None], axis=1), N_EXPERTS - 1).astype(jnp.int32)
    sub = MOE_ROWS // MOE_SUB_ROWS
    sub_block_e = jnp.repeat(block_e, sub)
    sorted_tok = (jnp.argsort(flat_e, stable=True) // TOP_K).astype(jnp.int32)
    sorted_tok = jnp.concatenate([sorted_tok, sorted_tok[:MOE_SUB_ROWS]])
    offset = jnp.cumsum(counts) - counts - (pad_end - padded)
    sel = (sub_block_e[:, None] == jnp.arange(N_EXPERTS, dtype=jnp.int32)[None, :]).astype(jnp.int32)
    sub_start = jnp.arange(n_blocks * sub, dtype=jnp.int32) * MOE_SUB_ROWS
    block_base = (jnp.sum(sel * offset[None, :], axis=1) + sub_start) % n_assign
    real_end = pad_end - padded + counts
    sub_valid = (sub_start < jnp.sum(sel * real_end[None, :], axis=1)).astype(jnp.int32)
    return dest, sorted_tok, block_base.astype(jnp.int32), block_e, sub_block_e, sub_valid


def kernel(x_prompt, x_sample, state_conv, cache_k, cache_v, p_prompt, p_sample, conv_w_pw1, conv_b_pw1, conv_w_dw, conv_b_dw, conv_ln_g, conv_ln_b, conv_w_pw2, conv_b_pw2, w_kv, rel_bias, attn_w_q, attn_sinks, attn_w_o, ln1_g, ln1_b, ln2_g, ln2_b, ffn_w_gate, ffn_w_up, ffn_w_down, moe_w_router, moe_b_router, moe_w_gate, moe_w_up, moe_w_down, ple_w_gate, ple_w_proj):
    n_pb, p_seq, _ = x_prompt.shape
    n_sb, s_seq, _ = x_sample.shape
    n_p, n_s = n_pb * p_seq, n_sb * s_seq
    vec = lambda v: v.reshape(1, -1).astype(F32)

    xp = x_prompt.reshape(n_p, D_MODEL)
    xs = x_sample.reshape(n_s, D_MODEL)
    w1 = conv_w_pw1[0].astype(BF16)
    b1 = vec(conv_b_pw1[0])
    u_p = _pw1_glu(xp, w1, b1)
    u_s = _pw1_glu(xs, w1, b1)
    w_dw, b_dw = conv_w_dw[0].astype(F32), vec(conv_b_dw[0])
    state = jnp.pad(state_conv[0], ((0, 0), (CONV_HALO - (CONV_WIDTH - 1), 0), (0, 0))).reshape(n_sb * CONV_HALO, D_MODEL)
    yc_p = _dwconv(u_p, u_p, w_dw, b_dw, n_seq=n_pb, seq=p_seq, bt=256, zero_first=True)
    yc_s = _dwconv(u_s, state, w_dw, b_dw, n_seq=n_sb, seq=s_seq, bt=s_seq, zero_first=False)
    w2 = conv_w_pw2[0].astype(BF16)
    tail0 = (w2, vec(conv_b_pw2[0]), vec(conv_ln_g[0]), vec(conv_ln_b[0]), vec(ln1_g[0]), vec(ln1_b[0]))
    h, hb = _pw2_ln1(yc_p, yc_s, xp, xs, *tail0)
    f = _ffn(hb, ffn_w_gate[0].astype(BF16), ffn_w_up[0].astype(BF16), ffn_w_down[0].astype(BF16))
    x1, x1b = _ple_ln2(h, hb, f, p_prompt[0].reshape(n_p, PLE_DIM), p_sample[0].reshape(n_s, PLE_DIM),
                       ple_w_gate[0].astype(BF16), ple_w_proj[0].astype(BF16), vec(ln2_g[0]), vec(ln2_b[0]))

    q = _mm(x1b, attn_w_q[0].astype(BF16), scale=HEAD_DIM ** -0.5, out_dtype=BF16)
    kv = _mm(x1b, w_kv.astype(BF16), scale=1.0, out_dtype=F32)
    bias = _rel_bias(rel_bias)
    sinks = jnp.repeat(attn_sinks[0].astype(F32), CHUNK).reshape(N_KV_HEADS, GROUP * CHUNK, 1)
    o_p = _attention(q, kv, kv, kv, bias, sinks, n_seq=n_pb, seq=p_seq, tq=256, row0=0, mask_first=True)
    o_s = _attention(q, kv, cache_k.reshape(n_sb * WINDOW, KV_DIM), cache_v.reshape(n_sb * WINDOW, KV_DIM),
                     bias, sinks, n_seq=n_sb, seq=s_seq, tq=s_seq, row0=n_p, mask_first=False)
    h1, h1p, route = _oproj_ln1_route(o_p, o_s, x1, attn_w_o[0].astype(BF16), vec(ln1_g[1]), vec(ln1_b[1]),
                                      moe_w_router[0], moe_b_router[0])

    dest, sorted_tok, block_base, block_e, sub_block_e, sub_valid = _route_plan(route)
    xg = _gather_rows(sorted_tok, block_base, h1p)
    a = _moe_up(block_e, sub_valid, xg, moe_w_gate[0], moe_w_up[0])
    ys = _moe_down(sub_block_e, sub_valid, a, moe_w_down[0])
    tail1 = (ple_w_gate[1].astype(BF16), ple_w_proj[1].astype(BF16), vec(ln2_g[1]), vec(ln2_b[1]))
    y_p = _combine_ple_ln2(dest, h1, route, p_prompt[1].reshape(n_p, PLE_DIM), ys, *tail1, row0=0, n_rows=n_p)
    y_s = _combine_ple_ln2(dest, h1, route, p_sample[1].reshape(n_s, PLE_DIM), ys, *tail1, row0=n_p, n_rows=n_s)

    keep = CONV_WIDTH - 1
    conv_p = u_p.reshape(n_pb, p_seq, D_MODEL)[None, :, p_seq - keep:]
    conv_s = u_s.reshape(n_sb, s_seq, D_MODEL)[None, :, s_seq - keep:]
    wps = p_seq // WINDOW
    kv_p = kv.reshape(-1, WINDOW, 2 * KV_DIM)[wps - 1:n_pb * wps:wps].reshape(n_pb, WINDOW, 2, N_KV_HEADS, HEAD_DIM)
    kv_s = kv[n_p:].reshape(n_sb, s_seq, 2, N_KV_HEADS, HEAD_DIM)
    k_s = jnp.concatenate([cache_k, kv_s[:, :, 0]], axis=1)[:, s_seq:]
    v_s = jnp.concatenate([cache_v, kv_s[:, :, 1]], axis=1)[:, s_seq:]
    return (y_p.reshape(n_pb, p_seq, D_MODEL), y_s.reshape(n_sb, s_seq, D_MODEL), conv_p, conv_s,
            kv_p[:, :, 0], kv_p[:, :, 1], k_s, v_s)
```
